```python
import math
import jax
import jax.numpy as jnp
from jax import lax
import numpy as np

D_MODEL = 2048
BATCH = 16
SEQ = 2048
DEPTH = 4

GRID_W = 64
CTX_LEN = 256
N_BRANCH = 4
W_BR = D_MODEL // N_BRANCH
CHUNK = 128
A_GROUPS = 4
A_GROUP_DIM = W_BR // A_GROUPS
B_GROUPS = 4
B_GROUP_DIM = W_BR // B_GROUPS
C_HEADS = 4
C_HEAD_DIM = W_BR // C_HEADS
WIN_H = 8
WIN_W = 16
S5_GROUP_CH = 16
S5_GROUPS = W_BR // S5_GROUP_CH
S5_STATE = 64
OFF_D_U = 0
OFF_C_K = OFF_D_U + W_BR
OFF_C_V = OFF_C_K + W_BR
OFF_C_Q = OFF_C_V + W_BR
OFF_A_UV = OFF_C_Q + W_BR
OFF_B = OFF_A_UV + 2 * W_BR
OFF_GATE = OFF_B + W_BR
OFF_MERGE = OFF_GATE + N_BRANCH * W_BR
IN_COLS = OFF_MERGE + N_BRANCH * D_MODEL
CTX_SIDE_COLS = OFF_C_Q

DEEPNORM_ALPHA = (2 * DEPTH) ** 0.25
DEEPNORM_BETA = (8 * DEPTH) ** -0.25
LN_EPS = 1e-6
NEG_INF = -1e30

kernel_name = 'hybrid_parallel_mixer_flow_backbone'


def _ln(x):
    xf = x.astype(jnp.float32)
    mu = jnp.mean(xf, axis=-1, keepdims=True)
    var = jnp.mean(jnp.square(xf - mu), axis=-1, keepdims=True)
    return ((xf - mu) * lax.rsqrt(var + LN_EPS)).astype(x.dtype)


def _cols(z, off, width):
    return z[..., off:off + width]


def _chunk_sgu(uv, w_s, b_s):
    bsz, n, _ = uv.shape
    u, v = jnp.split(jax.nn.gelu(uv), 2, axis=-1)
    v = _ln(v).reshape(bsz, n // CHUNK, CHUNK, A_GROUPS, A_GROUP_DIM)
    v = jnp.einsum('gpq,bkqgc->bkpgc', w_s, v) + b_s.T[:, :, None]
    return u * v.reshape(bsz, n, W_BR)


def _fourier_mix(z, w_f, b_f):
    bsz, n, _ = z.shape
    zg = z.astype(jnp.float32).reshape(bsz, n, B_GROUPS, B_GROUP_DIM)
    f = jnp.fft.fftn(zg, axes=(1, 3), norm='ortho').real.astype(z.dtype)
    return jnp.einsum('bngc,gcd->bngd', f, w_f).reshape(bsz, n, W_BR) + b_f


def _neighbourhood_attention(q, k, v, ck, cv, rpb):
    bsz, n, _ = q.shape
    rows = n // GRID_W
    kh = min(WIN_H, rows)
    kw = min(WIN_W, GRID_W)
    scale = C_HEAD_DIM ** -0.5
    q = q.reshape(bsz, rows, GRID_W, C_HEADS, C_HEAD_DIM)
    k = k.reshape(bsz, rows, GRID_W, C_HEADS, C_HEAD_DIM)
    v = v.reshape(bsz, rows, GRID_W, C_HEADS, C_HEAD_DIM)
    ck = ck.reshape(bsz, -1, C_HEADS, C_HEAD_DIM)
    cv = cv.reshape(bsz, -1, C_HEADS, C_HEAD_DIM)
    r = jnp.arange(rows)
    w = jnp.arange(GRID_W)
    row_start = jnp.clip(r - kh // 2, 0, rows - kh)
    row_idx = row_start[:, None] + jnp.arange(kh)[None, :]
    k_band = k[:, row_idx]
    v_band = v[:, row_idx]
    col_start = jnp.clip(w - kw // 2, 0, GRID_W - kw)
    in_win = (w[None, :] >= col_start[:, None]) & (w[None, :] < col_start[:, None] + kw)
    dr = row_idx - r[:, None]
    dc = jnp.clip(w[None, :] - w[:, None], -(kw - 1), kw - 1)
    bias = rpb.astype(jnp.float32)[:, dr + WIN_H - 1][..., dc + WIN_W - 1]
    bias = jnp.where(in_win[None, None, :, None, :], bias.transpose(0, 1, 3, 2, 4), NEG_INF)
    s_band = jnp.einsum('brqhd,brikhd->bhrqik', q, k_band).astype(jnp.float32) * scale + bias[None]
    s_ctx = jnp.einsum('brqhd,blhd->bhrql', q, ck).astype(jnp.float32) * scale
    n_band = kh * GRID_W
    s_all = jnp.concatenate([s_band.reshape(s_band.shape[:4] + (n_band,)), s_ctx], axis=-1)
    p = jax.nn.softmax(s_all, axis=-1).astype(v.dtype)
    p_band = p[..., :n_band].reshape(s_band.shape)
    p_ctx = p[..., n_band:]
    out = jnp.einsum('bhrqik,brikhd->brqhd', p_band, v_band) + jnp.einsum('bhrql,blhd->brqhd', p_ctx, cv)
    return out.reshape(bsz, n, W_BR)


def _context_attention(cq, ck, cv):
    bsz, lc, _ = cq.shape
    cq = cq.reshape(bsz, lc, C_HEADS, C_HEAD_DIM)
    ck = ck.reshape(bsz, lc, C_HEADS, C_HEAD_DIM)
    cv = cv.reshape(bsz, lc, C_HEADS, C_HEAD_DIM)
    s = jnp.einsum('blhd,bmhd->bhlm', cq, ck).astype(jnp.float32) * (C_HEAD_DIM ** -0.5)
    p = jax.nn.softmax(s, axis=-1).astype(cv.dtype)
    return jnp.einsum('bhlm,bmhd->blhd', p, cv).reshape(bsz, lc, W_BR)


def _cmul(ar, ai, br, bi):
    return ar * br - ai * bi, ar * bi + ai * br


def _scan_combine(e1, e2):
    a1r, a1i, b1r, b1i = e1
    a2r, a2i, b2r, b2i = e2
    ar, ai = _cmul(a1r, a1i, a2r, a2i)
    br, bi = _cmul(a2r, a2i, b1r, b1i)
    return ar, ai, br + b2r, bi + b2i


def _s5_scan(u, a_re, a_im, log_dt, b_re, b_im, reverse, h0=None):
    dt = jnp.exp(log_dt)[:, None]
    lr, li = a_re * dt, a_im * dt
    mag = jnp.exp(lr)
    abar_re, abar_im = mag * jnp.cos(li), mag * jnp.sin(li)
    den = a_re * a_re + a_im * a_im
    f_re = ((abar_re - 1.0) * a_re + abar_im * a_im) / den
    f_im = (abar_im * a_re - (abar_re - 1.0) * a_im) / den
    bb_re = f_re[..., None] * b_re - f_im[..., None] * b_im
    bb_im = f_re[..., None] * b_im + f_im[..., None] * b_re
    bu_re = jnp.einsum('bngh,gph->bngp', u, bb_re)
    bu_im = jnp.einsum('bngh,gph->bngp', u, bb_im)
    n = u.shape[1]
    a_seq_re = jnp.broadcast_to(abar_re, (1, n) + abar_re.shape)
    a_seq_im = jnp.broadcast_to(abar_im, (1, n) + abar_im.shape)
    _, _, x_re, x_im = lax.associative_scan(_scan_combine, (a_seq_re, a_seq_im, bu_re, bu_im), axis=1, reverse=reverse)
    if h0 is not None:
        steps = jnp.arange(n, 0, -1) if reverse else jnp.arange(1, n + 1)
        steps = steps.astype(jnp.float32)[:, None, None]
        pm = jnp.exp(steps * lr)
        p_re, p_im = pm * jnp.cos(steps * li), pm * jnp.sin(steps * li)
        h_re, h_im = h0[0][:, None], h0[1][:, None]
        x_re = x_re + p_re * h_re - p_im * h_im
        x_im = x_im + p_re * h_im + p_im * h_re
    return x_re, x_im


def _s5_readout(xs, c_re, c_im):
    return jnp.einsum('bngp,ghp->bngh', xs[0], c_re) - jnp.einsum('bngp,ghp->bngh', xs[1], c_im)


def _s5_branch(ux, uc, a_re, a_im, log_dt, b_re, b_im, c_re, c_im, d, w_glu, b_glu, need_ctx):
    dtype = ux.dtype
    f32 = jnp.float32
    a_re, a_im, log_dt = a_re.astype(f32), a_im.astype(f32), log_dt.astype(f32)
    b_re, b_im, c_re, c_im = b_re.astype(f32), b_im.astype(f32), c_re.astype(f32), c_im.astype(f32)
    bsz, n, _ = ux.shape
    ux_g = ux.astype(f32).reshape(bsz, n, S5_GROUPS, S5_GROUP_CH)
    uc_g = uc.astype(f32).reshape(bsz, uc.shape[1], S5_GROUPS, S5_GROUP_CH)
    fwd = (a_re[0], a_im[0], log_dt[0], b_re[0], b_im[0])
    bwd = (a_re[1], a_im[1], log_dt[1], b_re[1], b_im[1])
    xcf = _s5_scan(uc_g, *fwd, reverse=False)
    xcb = _s5_scan(uc_g, *bwd, reverse=True)
    xf = _s5_scan(ux_g, *fwd, reverse=False, h0=(xcf[0][:, -1], xcf[1][:, -1]))
    xb = _s5_scan(ux_g, *bwd, reverse=True, h0=(xcb[0][:, 0], xcb[1][:, 0]))

    def emit(u_g, sf, sb):
        y = _s5_readout(sf, c_re[0], c_im[0]) + _s5_readout(sb, c_re[1], c_im[1])
        y = y.reshape(bsz, -1, W_BR) + d.astype(f32) * u_g.reshape(bsz, -1, W_BR)
        y = jax.nn.gelu(y).astype(dtype)
        return y * jax.nn.sigmoid(y @ w_glu + b_glu)

    y_ctx = emit(uc_g, xcf, xcb) if need_ctx else None
    return emit(ux_g, xf, xb), y_ctx


def _merge_residual(x, z, ys, gate, w_up, w_o, ln_g, ln_b):
    m = None
    for i, y in enumerate(ys):
        y = y * jax.nn.silu(_cols(z, OFF_GATE + i * W_BR, W_BR))
        term = jax.nn.sigmoid(_cols(z, OFF_MERGE + i * D_MODEL, D_MODEL)) * (y @ w_up[i])
        m = term if m is None else m + term
    out = m @ w_o
    return _ln(DEEPNORM_ALPHA * x + gate * out) * ln_g + ln_b


def setup_inputs(seed: int = 0) -> dict:
    key = jax.random.key(seed)
    ks = jax.random.split(key, 28)
    f32 = jnp.float32
    L, G, P, H = DEPTH, S5_GROUPS, S5_STATE, S5_GROUP_CH

    def nrm(k, shape, s):
        return jax.random.normal(k, shape, f32) * s

    n_idx = jnp.arange(P, dtype=f32)
    return {
        'x': nrm(ks[0], (BATCH, SEQ, D_MODEL), 1.0),
        'c': nrm(ks[1], (BATCH, D_MODEL), 1.0),
        'ctx': nrm(ks[2], (BATCH, CTX_LEN, D_MODEL), 1.0),
        'c_ctx': nrm(ks[3], (D_MODEL,), 1.0),
        'w_ada': nrm(ks[4], (L, D_MODEL, 3 * D_MODEL), D_MODEL ** -0.5),
        'b_ada': nrm(ks[5], (L, 3 * D_MODEL), 0.02),
        'w_in': nrm(ks[6], (L, D_MODEL, IN_COLS), D_MODEL ** -0.5),
        'b_in': nrm(ks[7], (L, IN_COLS), 0.02),
        'w_sgu': nrm(ks[8], (L, A_GROUPS, CHUNK, CHUNK), CHUNK ** -0.5),
        'b_sgu': 1.0 + nrm(ks[9], (L, A_GROUPS, CHUNK), 0.1),
        'w_fnet': nrm(ks[10], (L, B_GROUPS, B_GROUP_DIM, B_GROUP_DIM), B_GROUP_DIM ** -0.5),
        'b_fnet': nrm(ks[11], (L, W_BR), 0.02),
        'rpb': nrm(ks[12], (L, C_HEADS, 2 * WIN_H - 1, 2 * WIN_W - 1), 0.1),
        's5_a_re': -0.5 + nrm(ks[13], (L, 2, G, P), 0.01),
        's5_a_im': math.pi * n_idx + nrm(ks[14], (L, 2, G, P), 0.01),
        's5_log_dt': jax.random.uniform(ks[15], (L, 2, G), f32, math.log(1e-3), math.log(1e-1)),
        's5_b_re': nrm(ks[16], (L, 2, G, P, H), (2 * H) ** -0.5),
        's5_b_im': nrm(ks[17], (L, 2, G, P, H), (2 * H) ** -0.5),
        's5_c_re': nrm(ks[18], (L, 2, G, H, P), P ** -0.5),
        's5_c_im': nrm(ks[19], (L, 2, G, H, P), P ** -0.5),
        's5_d': nrm(ks[20], (L, W_BR), 1.0),
        's5_w_glu': nrm(ks[21], (L, W_BR, W_BR), W_BR ** -0.5),
        's5_b_glu': nrm(ks[22], (L, W_BR), 0.02),
        'w_up': nrm(ks[23], (L, N_BRANCH, W_BR, D_MODEL), DEEPNORM_BETA * W_BR ** -0.5),
        'w_o': nrm(ks[24], (L, D_MODEL, D_MODEL), DEEPNORM_BETA * D_MODEL ** -0.5),
        'ln_g': 1.0 + nrm(ks[25], (L, D_MODEL), 0.02),
        'ln_b': nrm(ks[26], (L, D_MODEL), 0.02),
    }


def reference(x, c, ctx, c_ctx, w_ada, b_ada, w_in, b_in, w_sgu, b_sgu, w_fnet, b_fnet, rpb,
              s5_a_re, s5_a_im, s5_log_dt, s5_b_re, s5_b_im, s5_c_re, s5_c_im, s5_d, s5_w_glu, s5_b_glu,
              w_up, w_o, ln_g, ln_b):
    silu_c = jax.nn.silu(c)
    silu_cc = jax.nn.silu(c_ctx)
    for l in range(DEPTH):
        need_ctx = l < DEPTH - 1
        ada_x = silu_c @ w_ada[l] + b_ada[l]
        ada_c = silu_cc @ w_ada[l] + b_ada[l]
        shift_x, scale_x, gate_x = jnp.split(ada_x[:, None, :], 3, axis=-1)
        shift_c, scale_c, gate_c = jnp.split(ada_c, 3, axis=-1)
        hx = _ln(x) * (1.0 + scale_x) + shift_x
        hc = _ln(ctx) * (1.0 + scale_c) + shift_c
        cols = IN_COLS if need_ctx else CTX_SIDE_COLS
        zx = hx @ w_in[l] + b_in[l]
        zc = hc @ w_in[l][:, :cols] + b_in[l][:cols]

        yd_x, yd_c = _s5_branch(_cols(zx, OFF_D_U, W_BR), _cols(zc, OFF_D_U, W_BR),
                                s5_a_re[l], s5_a_im[l], s5_log_dt[l], s5_b_re[l], s5_b_im[l],
                                s5_c_re[l], s5_c_im[l], s5_d[l], s5_w_glu[l], s5_b_glu[l], need_ctx)
        ck, cv = _cols(zc, OFF_C_K, W_BR), _cols(zc, OFF_C_V, W_BR)
        yc_x = _neighbourhood_attention(_cols(zx, OFF_C_Q, W_BR), _cols(zx, OFF_C_K, W_BR),
                                        _cols(zx, OFF_C_V, W_BR), ck, cv, rpb[l])
        ya_x = _chunk_sgu(_cols(zx, OFF_A_UV, 2 * W_BR), w_sgu[l], b_sgu[l])
        yb_x = _fourier_mix(_cols(zx, OFF_B, W_BR), w_fnet[l], b_fnet[l])
        x_new = _merge_residual(x, zx, (ya_x, yb_x, yc_x, yd_x), gate_x, w_up[l], w_o[l], ln_g[l], ln_b[l])
        if need_ctx:
            yc_c = _context_attention(_cols(zc, OFF_C_Q, W_BR), ck, cv)
            ya_c = _chunk_sgu(_cols(zc, OFF_A_UV, 2 * W_BR), w_sgu[l], b_sgu[l])
            yb_c = _fourier_mix(_cols(zc, OFF_B, W_BR), w_fnet[l], b_fnet[l])
            ctx = _merge_residual(ctx, zc, (ya_c, yb_c, yc_c, yd_c), gate_c, w_up[l], w_o[l], ln_g[l], ln_b[l])
        x = x_new
    return x
```

```python
import functools
import math

import jax
import jax.numpy as jnp
import numpy as np
from jax import lax
from jax.experimental import pallas as pl
from jax.experimental.pallas import tpu as pltpu

GRID_W = 64
N_BRANCH = 4
LN_EPS = 1e-6
NEG_INF = -1e30
Q_ROWS = 4
N_SEG = 8
LANES = 128
VMEM_LIMIT = 56 * 1024 * 1024

F32 = jnp.float32
BF16 = jnp.bfloat16


def _cparams(sem):
    return pltpu.CompilerParams(dimension_semantics=sem, vmem_limit_bytes=VMEM_LIMIT)


def _dot(a, b):
    return jnp.dot(a, b, preferred_element_type=F32)


def _dot_t(a, b):
    return lax.dot_general(a, b, (((1,), (1,)), ((), ())), preferred_element_type=F32)


def _ln(x):
    mu = jnp.mean(x, axis=-1, keepdims=True)
    xc = x - mu
    var = jnp.mean(xc * xc, axis=-1, keepdims=True)
    return xc * lax.rsqrt(var + LN_EPS)


def _silu(x):
    return x * jax.nn.sigmoid(x)


def _pick_tile(n, target):
    t = min(n, target)
    while n % t:
        t //= 2
    return t


def _ada_kernel(s_ref, w_ref, b_ref, o_ref):
    s = _silu(s_ref[...]).astype(BF16)
    o_ref[0] = _dot(s, w_ref[0].astype(BF16)) + b_ref[0]


def _ada_all(cond, w_ada, b_ada):
    depth, d, d3 = w_ada.shape
    r = cond.shape[0]
    tn = _pick_tile(d3, 1024)
    return pl.pallas_call(
        _ada_kernel,
        grid=(depth, d3 // tn),
        in_specs=[
            pl.BlockSpec((r, d), lambda l, j: (0, 0)),
            pl.BlockSpec((1, d, tn), lambda l, j: (l, 0, j)),
            pl.BlockSpec((1, 1, tn), lambda l, j: (l, 0, j)),
        ],
        out_specs=pl.BlockSpec((1, r, tn), lambda l, j: (l, 0, j)),
        out_shape=jax.ShapeDtypeStruct((depth, r, d3), F32),
        compiler_params=_cparams(("arbitrary", "arbitrary")),
        name="ada_proj",
    )(cond, w_ada, b_ada.reshape(depth, 1, d3))


def _inproj_kernel(x_ref, mod_ref, w_ref, b_ref, o_ref, h_ref, *, d):
    @pl.when(pl.program_id(1) == 0)
    def _():
        shift = mod_ref[0, :, 0:d]
        scale = mod_ref[0, :, d:2 * d]
        h_ref[...] = (_ln(x_ref[...]) * (1.0 + scale) + shift).astype(BF16)

    o_ref[...] = _dot(h_ref[...], w_ref[...]) + b_ref[...]


def _in_proj(x2, mod, w, b, ncols, tm, tn):
    t, d = x2.shape
    n_mod = mod.shape[0]
    tiles_per_mod = (t // n_mod) // tm
    return pl.pallas_call(
        functools.partial(_inproj_kernel, d=d),
        grid=(t // tm, ncols // tn),
        in_specs=[
            pl.BlockSpec((tm, d), lambda i, j: (i, 0)),
            pl.BlockSpec((1, 1, 3 * d), lambda i, j: (i // tiles_per_mod, 0, 0)),
            pl.BlockSpec((d, tn), lambda i, j: (0, j)),
            pl.BlockSpec((1, tn), lambda i, j: (0, j)),
        ],
        out_specs=pl.BlockSpec((tm, tn), lambda i, j: (i, j)),
        out_shape=jax.ShapeDtypeStruct((t, ncols), F32),
        scratch_shapes=[pltpu.VMEM((tm, d), BF16)],
        compiler_params=_cparams(("parallel", "arbitrary")),
        name="in_proj",
    )(x2, mod, w, b)


def _sgu_kernel(uv_ref, g_ref, w_ref, b_ref, o_ref, *, wbr, chunk, groups):
    uv = jax.nn.gelu(uv_ref[0])
    u = uv[:, :wbr]
    v = _ln(uv[:, wbr:]).astype(BF16)
    gate = _silu(g_ref[0])
    gd = wbr // groups
    for k in range(uv.shape[0] // chunk):
        rows = slice(k * chunk, (k + 1) * chunk)
        for g in range(groups):
            cols = slice(g * gd, (g + 1) * gd)
            mixed = _dot(w_ref[g], v[rows, cols]) + b_ref[g]
            o_ref[0, rows, cols] = (u[rows, cols] * mixed * gate[rows, cols]).astype(BF16)


def _sgu(z3, w_sgu, b_sgu, wbr, uv_blk, gate_blk, ta):
    bsz, n, _ = z3.shape
    groups, chunk, _ = w_sgu.shape
    return pl.pallas_call(
        functools.partial(_sgu_kernel, wbr=wbr, chunk=chunk, groups=groups),
        grid=(bsz, n // ta),
        in_specs=[
            pl.BlockSpec((1, ta, 2 * wbr), lambda b, i: (b, i, uv_blk)),
            pl.BlockSpec((1, ta, wbr), lambda b, i: (b, i, gate_blk)),
            pl.BlockSpec((groups, chunk, chunk), lambda b, i: (0, 0, 0)),
            pl.BlockSpec((groups, chunk, 1), lambda b, i: (0, 0, 0)),
        ],
        out_specs=pl.BlockSpec((1, ta, wbr), lambda b, i: (b, i, 0)),
        out_shape=jax.ShapeDtypeStruct((bsz, n, wbr), BF16),
        compiler_params=_cparams(("parallel", "arbitrary")),
        name="sgu_branch",
    )(z3, z3, w_sgu.astype(BF16), b_sgu.reshape(groups, chunk, 1))


def _dft_tables(n):
    k = jnp.arange(n, dtype=jnp.int32)
    r = (k[:, None] * k[None, :]) % n
    ang = r.astype(F32) * (2.0 * math.pi / n)
    return jnp.cos(ang).astype(BF16), jnp.sin(ang).astype(BF16)


def _fnet_kernel(z_ref, g_ref, cc_ref, sc_ref, cn_ref, sn_ref, wf_ref, bf_ref, o_ref, p_ref, q_ref,
                 *, groups, norm):
    wbr = z_ref.shape[2]
    gd = wbr // groups

    @pl.when(pl.program_id(1) == 0)
    def _():
        for g in range(groups):
            cols = slice(g * gd, (g + 1) * gd)
            zg = z_ref[0, :, cols].astype(BF16)
            p_ref[:, cols] = _dot(zg, cc_ref[...]).astype(BF16)
            q_ref[:, cols] = _dot(zg, sc_ref[...]).astype(BF16)

    f = (_dot(cn_ref[...], p_ref[...]) - _dot(sn_ref[...], q_ref[...])) * norm
    gate = _silu(g_ref[0])
    for g in range(groups):
        cols = slice(g * gd, (g + 1) * gd)
        y = _dot(f[:, cols].astype(BF16), wf_ref[g]) + bf_ref[:, cols]
        o_ref[0, :, cols] = (y * gate[:, cols]).astype(BF16)


def _fnet(z3, w_f, b_f, tables, wbr, z_blk, gate_blk, tk):
    bsz, n, _ = z3.shape
    groups, gd, _ = w_f.shape
    cn, sn, cc, sc = tables
    return pl.pallas_call(
        functools.partial(_fnet_kernel, groups=groups, norm=1.0 / math.sqrt(n * gd)),
        grid=(bsz, n // tk),
        in_specs=[
            pl.BlockSpec((1, n, wbr), lambda b, k: (b, 0, z_blk)),
            pl.BlockSpec((1, tk, wbr), lambda b, k: (b, k, gate_blk)),
            pl.BlockSpec((gd, gd), lambda b, k: (0, 0)),
            pl.BlockSpec((gd, gd), lambda b, k: (0, 0)),
            pl.BlockSpec((tk, n), lambda b, k: (k, 0)),
            pl.BlockSpec((tk, n), lambda b, k: (k, 0)),
            pl.BlockSpec((groups, gd, gd), lambda b, k: (0, 0, 0)),
            pl.BlockSpec((1, wbr), lambda b, k: (0, 0)),
        ],
        out_specs=pl.BlockSpec((1, tk, wbr), lambda b, k: (b, k, 0)),
        out_shape=jax.ShapeDtypeStruct((bsz, n, wbr), BF16),
        scratch_shapes=[pltpu.VMEM((n, wbr), BF16), pltpu.VMEM((n, wbr), BF16)],
        compiler_params=_cparams(("parallel", "arbitrary")),
        name="fnet_branch",
    )(z3, z3, cc, sc, cn, sn, w_f.astype(BF16), b_f.reshape(1, wbr))


def _nattn_plan(rows, width, win_h, win_w):
    kh, kw = min(win_h, rows), min(win_w, width)
    qb = Q_ROWS
    wr = min(kh + qb - 1, rows)
    nblk = rows // qb
    r0 = np.arange(nblk) * qb
    ks = np.clip(r0 - kh // 2, 0, rows - wr)
    r = r0[:, None] + np.arange(qb)[None]
    rs = np.clip(r - kh // 2, 0, rows - kh)
    kr = ks[:, None] + np.arange(wr)[None]
    row_ok = (kr[:, None, :] >= rs[:, :, None]) & (kr[:, None, :] < rs[:, :, None] + kh)
    assert row_ok.sum(-1).min() == kh
    dri = np.where(row_ok, kr[:, None, :] - r[:, :, None] + win_h - 1, 0)
    key = np.concatenate([dri.reshape(nblk, -1), row_ok.reshape(nblk, -1)], axis=1)
    _, first, inv = np.unique(key, axis=0, return_index=True, return_inverse=True)
    qc = np.arange(width)
    col_start = np.clip(qc - kw // 2, 0, width - kw)
    in_win = (qc[None, :] >= col_start[:, None]) & (qc[None, :] < col_start[:, None] + kw)
    dci = np.clip(qc[None, :] - qc[:, None], -(kw - 1), kw - 1) + win_w - 1
    return dict(qb=qb, wr=wr, nblk=nblk, ks=ks.astype(np.int32), pat=inv.reshape(-1).astype(np.int32),
                dri=dri[first], row_ok=row_ok[first], in_win=in_win, dci=dci)


def _nattn_bias(rpb, plan, width):
    dri, row_ok = plan["dri"], plan["row_ok"]
    npat, qb, wr = dri.shape
    idx_r = jnp.asarray(dri)[:, :, None, :, None]
    idx_c = jnp.asarray(plan["dci"])[None, None, :, None, :]
    vals = rpb.astype(F32)[:, idx_r, idx_c]
    valid = row_ok[:, :, None, :, None] & plan["in_win"][None, None, :, None, :]
    vals = jnp.where(jnp.asarray(valid)[None], vals, NEG_INF)
    heads = rpb.shape[0]
    return vals.transpose(1, 0, 2, 3, 4, 5).reshape(npat, heads, qb * width, wr * width)


def _softmax_pv(s_band, s_ctx, v_band, v_ctx):
    m = jnp.maximum(jnp.max(s_band, axis=-1, keepdims=True), jnp.max(s_ctx, axis=-1, keepdims=True))
    p_band = jnp.exp(s_band - m)
    p_ctx = jnp.exp(s_ctx - m)
    den = jnp.sum(p_band, axis=-1, keepdims=True) + jnp.sum(p_ctx, axis=-1, keepdims=True)
    out = _dot(p_band.astype(BF16), v_band) + _dot(p_ctx.astype(BF16), v_ctx)
    return out / den


def _nattn_kernel(ks_ref, pat_ref, q_ref, k_ref, v_ref, ck_ref, cv_ref, g_ref, bias_ref, o_ref,
                  *, heads, width, nkeys):
    blk = pl.program_id(1)
    start = pl.multiple_of(ks_ref[blk] * width, width)
    pat = pat_ref[blk]
    dh = q_ref.shape[2] // heads
    scale = dh ** -0.5
    gate = _silu(g_ref[0])
    for h in range(heads):
        cols = slice(h * dh, (h + 1) * dh)
        q = q_ref[0, :, cols].astype(BF16)
        kb = k_ref[0, pl.ds(start, nkeys), cols].astype(BF16)
        vb = v_ref[0, pl.ds(start, nkeys), cols].astype(BF16)
        s_band = _dot_t(q, kb) * scale + bias_ref[pat, h]
        s_ctx = _dot_t(q, ck_ref[0, :, cols].astype(BF16)) * scale
        out = _softmax_pv(s_band, s_ctx, vb, cv_ref[0, :, cols].astype(BF16))
        o_ref[0, :, cols] = (out * gate[:, cols]).astype(BF16)


def _nattn(zx3, zc3, bias, plan, heads, wbr, blks, gate_blk):
    bsz, n, _ = zx3.shape
    lc = zc3.shape[1]
    k_blk, v_blk, q_blk = blks
    nq = plan["qb"] * GRID_W
    nkeys = plan["wr"] * GRID_W
    grid_spec = pltpu.PrefetchScalarGridSpec(
        num_scalar_prefetch=2,
        grid=(bsz, plan["nblk"]),
        in_specs=[
            pl.BlockSpec((1, nq, wbr), lambda b, i, *_: (b, i, q_blk)),
            pl.BlockSpec((1, n, wbr), lambda b, i, *_: (b, 0, k_blk)),
            pl.BlockSpec((1, n, wbr), lambda b, i, *_: (b, 0, v_blk)),
            pl.BlockSpec((1, lc, wbr), lambda b, i, *_: (b, 0, k_blk)),
            pl.BlockSpec((1, lc, wbr), lambda b, i, *_: (b, 0, v_blk)),
            pl.BlockSpec((1, nq, wbr), lambda b, i, *_: (b, i, gate_blk)),
            pl.BlockSpec(bias.shape, lambda b, i, *_: (0, 0, 0, 0)),
        ],
        out_specs=pl.BlockSpec((1, nq, wbr), lambda b, i, *_: (b, i, 0)),
    )
    return pl.pallas_call(
        functools.partial(_nattn_kernel, heads=heads, width=GRID_W, nkeys=nkeys),
        grid_spec=grid_spec,
        out_shape=jax.ShapeDtypeStruct((bsz, n, wbr), BF16),
        compiler_params=_cparams(("parallel", "arbitrary")),
        name="nattn_branch",
    )(jnp.asarray(plan["ks"]), jnp.asarray(plan["pat"]), zx3, zx3, zx3, zc3, zc3, zx3, bias)


def _cattn_kernel(q_ref, k_ref, v_ref, g_ref, o_ref, *, heads):
    dh = q_ref.shape[2] // heads
    scale = dh ** -0.5
    gate = _silu(g_ref[0])
    for h in range(heads):
        cols = slice(h * dh, (h + 1) * dh)
        s = _dot_t(q_ref[0, :, cols].astype(BF16), k_ref[0, :, cols].astype(BF16)) * scale
        p = jnp.exp(s - jnp.max(s, axis=-1, keepdims=True))
        out = _dot(p.astype(BF16), v_ref[0, :, cols].astype(BF16)) / jnp.sum(p, axis=-1, keepdims=True)
        o_ref[0, :, cols] = (out * gate[:, cols]).astype(BF16)


def _cattn(zc3, heads, wbr, blks, gate_blk):
    bsz, lc, _ = zc3.shape
    k_blk, v_blk, q_blk = blks
    spec = lambda blk: pl.BlockSpec((1, lc, wbr), lambda b: (b, 0, blk))
    return pl.pallas_call(
        functools.partial(_cattn_kernel, heads=heads),
        grid=(bsz,),
        in_specs=[spec(q_blk), spec(k_blk), spec(v_blk), spec(gate_blk)],
        out_specs=spec(0),
        out_shape=jax.ShapeDtypeStruct((bsz, lc, wbr), BF16),
        compiler_params=_cparams(("parallel",)),
        name="cattn_branch",
    )(zc3, zc3, zc3, zc3)


def _s5_discretise(a_re, a_im, log_dt, b_re, b_im):
    dt = jnp.exp(log_dt)[..., None]
    lr, li = a_re * dt, a_im * dt
    mag = jnp.exp(lr)
    abar_re, abar_im = mag * jnp.cos(li), mag * jnp.sin(li)
    den = a_re * a_re + a_im * a_im
    f_re = ((abar_re - 1.0) * a_re + abar_im * a_im) / den
    f_im = (abar_im * a_re - (abar_re - 1.0) * a_im) / den
    bb_re = f_re[..., None] * b_re - f_im[..., None] * b_im
    bb_im = f_re[..., None] * b_im + f_im[..., None] * b_re
    return lr, li, bb_re, bb_im


def _block_diag(m, gpb):
    two, g, a, b = m.shape
    nb = g // gpb
    m = m.reshape(two, nb, gpb, a, b)
    eye = jnp.eye(gpb, dtype=m.dtype)
    out = m[:, :, :, :, None, :] * eye[None, None, :, None, :, None]
    return out.reshape(two, nb, gpb * a, gpb * b)


def _s5_params(a_re, a_im, log_dt, b_re, b_im, c_re, c_im, seg_lens):
    f32 = lambda t: t.astype(F32)
    a_re, a_im, log_dt, b_re, b_im, c_re, c_im = map(f32, (a_re, a_im, log_dt, b_re, b_im, c_re, c_im))
    _, g, p, h = b_re.shape
    gpb = min(g, max(1, LANES // h))
    lr, li, bb_re, bb_im = _s5_discretise(a_re, a_im, log_dt, b_re, b_im)
    tr = lambda t: jnp.swapaxes(t, -1, -2)
    bmat = jnp.concatenate([_block_diag(tr(bb_re), gpb), _block_diag(tr(bb_im), gpb)], axis=-1)
    cmat = jnp.stack([_block_diag(tr(c_re), gpb), _block_diag(-tr(c_im), gpb)], axis=2)
    lr, li = lr.reshape(2, 1, g * p), li.reshape(2, 1, g * p)

    def power(k):
        m = jnp.exp(k * lr)
        return jnp.concatenate([m * jnp.cos(k * li), m * jnp.sin(k * li)], axis=-1)

    tables = {}
    for seg in seg_lens:
        steps = jnp.arange(1, seg + 1, dtype=F32)[None, :, None]
        steps = jnp.concatenate([steps, steps[:, ::-1]], axis=0)
        tables[seg] = power(steps)
    return dict(bmat=bmat.astype(BF16), cmat=cmat.astype(BF16), abar=power(1.0), tables=tables, gpb=gpb)


def _cmul_add(ar, ai, xr, xi, br, bi):
    return ar * xr - ai * xi + br, ar * xi + ai * xr + bi


def _s5_kernel(u_ref, g_ref, h0_ref, bm_ref, cm_ref, a_ref, pw_ref, d_ref, wg_ref, bg_ref,
               o_ref, ht_ref, xs_ref, xb_ref, y_ref, cs_ref, carry_ref, *, nchunks, seg, tiles_per_blk):
    dirn = pl.program_id(1)
    k = pl.program_id(2)
    tc, wbr = u_ref.shape[1], u_ref.shape[2]
    n_tiles, lt = xs_ref.shape[0] // 2, xs_ref.shape[2]
    sp = n_tiles * lt
    nb = bm_ref.shape[1]
    bw_in, bw_st = wbr // nb, sp // nb
    chunk = jnp.where(dirn == 0, k, nchunks - 1 - k)
    row0 = pl.multiple_of(chunk * tc, tc)
    re_cols = lambda t: slice(t * lt, (t + 1) * lt)
    im_cols = lambda t: slice(sp + t * lt, sp + (t + 1) * lt)

    @pl.when(k == 0)
    def _():
        carry_ref[...] = h0_ref[0, 0]

    u = u_ref[0]
    ub = u.astype(BF16)
    for n in range(nb):
        bu = _dot(ub[:, n * bw_in:(n + 1) * bw_in], bm_ref[0, n])
        for t in range(bw_st // lt):
            xs_ref[n * (bw_st // lt) + t] = bu[:, t * lt:(t + 1) * lt]
            xs_ref[n_tiles + n * (bw_st // lt) + t] = bu[:, bw_st + t * lt:bw_st + (t + 1) * lt]

    def scan(reverse):
        for q in range(n_tiles // tiles_per_blk):
            tiles = range(q * tiles_per_blk, (q + 1) * tiles_per_blk)
            ar = [jnp.broadcast_to(a_ref[0, :, re_cols(t)], (N_SEG, lt)) for t in tiles]
            ai = [jnp.broadcast_to(a_ref[0, :, im_cols(t)], (N_SEG, lt)) for t in tiles]

            def step(i, st):
                j = seg - 1 - i if reverse else i
                rows = pl.ds(j, N_SEG, stride=seg)
                out = []
                for idx, t in enumerate(tiles):
                    xr, xi = _cmul_add(ar[idx], ai[idx], st[2 * idx], st[2 * idx + 1],
                                       xs_ref[t, rows, :], xs_ref[n_tiles + t, rows, :])
                    xs_ref[t, rows, :] = xr
                    xs_ref[n_tiles + t, rows, :] = xi
                    out += [xr, xi]
                return tuple(out)

            zero = jnp.zeros((N_SEG, lt), F32)
            lax.fori_loop(0, seg, step, (zero,) * (2 * tiles_per_blk))

    @pl.when(dirn == 0)
    def _():
        scan(False)

    @pl.when(dirn == 1)
    def _():
        scan(True)

    def carries(order, end_row, pow_row):
        for t in range(n_tiles):
            alr = pw_ref[0, pow_row:pow_row + 1, re_cols(t)]
            ali = pw_ref[0, pow_row:pow_row + 1, im_cols(t)]
            cr, ci = carry_ref[:, re_cols(t)], carry_ref[:, im_cols(t)]
            for s in order:
                cs_ref[s:s + 1, re_cols(t)] = cr
                cs_ref[s:s + 1, im_cols(t)] = ci
                e = s * seg + end_row
                cr, ci = _cmul_add(alr, ali, cr, ci, xs_ref[t, e:e + 1, :], xs_ref[n_tiles + t, e:e + 1, :])
            carry_ref[:, re_cols(t)] = cr
            carry_ref[:, im_cols(t)] = ci

    @pl.when(dirn == 0)
    def _():
        carries(range(N_SEG), seg - 1, seg - 1)

    @pl.when(dirn == 1)
    def _():
        carries(range(N_SEG - 1, -1, -1), 0, 0)

    ht_ref[0, 0] = carry_ref[...]

    for s in range(N_SEG):
        rows = slice(s * seg, (s + 1) * seg)
        for t in range(n_tiles):
            xr, xi = _cmul_add(pw_ref[0, :, re_cols(t)], pw_ref[0, :, im_cols(t)],
                               cs_ref[s:s + 1, re_cols(t)], cs_ref[s:s + 1, im_cols(t)],
                               xs_ref[t, rows, :], xs_ref[n_tiles + t, rows, :])
            xb_ref[rows, re_cols(t)] = xr.astype(BF16)
            xb_ref[rows, im_cols(t)] = xi.astype(BF16)

    parts = []
    for n in range(nb):
        parts.append(_dot(xb_ref[:, n * bw_st:(n + 1) * bw_st], cm_ref[0, n, 0])
                     + _dot(xb_ref[:, sp + n * bw_st:sp + (n + 1) * bw_st], cm_ref[0, n, 1]))
    y = parts[0] if nb == 1 else jnp.concatenate(parts, axis=-1)

    @pl.when(dirn == 0)
    def _():
        y_ref[pl.ds(row0, tc), :] = y

    @pl.when(dirn == 1)
    def _():
        tot = jax.nn.gelu(y_ref[pl.ds(row0, tc), :] + y + d_ref[...] * u)
        glu = jax.nn.sigmoid(_dot(tot.astype(BF16), wg_ref[...]) + bg_ref[...])
        o_ref[0] = (tot * glu * _silu(g_ref[0])).astype(BF16)


def _s5(z3, h0, prm, d, w_glu, b_glu, wbr, u_blk, gate_blk, tc):
    bsz, n, _ = z3.shape
    nchunks = n // tc
    seg = tc // N_SEG
    sp2 = prm["abar"].shape[-1]
    bmat, cmat = prm["bmat"], prm["cmat"]
    sp = sp2 // 2
    lt = min(LANES, sp // bmat.shape[1])
    tiles_per_blk = min(sp // lt, 4)

    def chunk_of(dirn, k):
        return jnp.where(dirn == 0, k, nchunks - 1 - k)

    def out_chunk(dirn, k):
        return jnp.where(dirn == 0, nchunks - 1, nchunks - 1 - k)

    dir_spec = lambda shape: pl.BlockSpec((1,) + shape[1:], lambda b, dr, k: (dr,) + (0,) * (len(shape) - 1))
    return pl.pallas_call(
        functools.partial(_s5_kernel, nchunks=nchunks, seg=seg, tiles_per_blk=tiles_per_blk),
        grid=(bsz, 2, nchunks),
        in_specs=[
            pl.BlockSpec((1, tc, wbr), lambda b, dr, k: (b, chunk_of(dr, k), u_blk)),
            pl.BlockSpec((1, tc, wbr), lambda b, dr, k: (b, out_chunk(dr, k), gate_blk)),
            pl.BlockSpec((1, 1, 1, sp2), lambda b, dr, k: (b, dr, 0, 0)),
            dir_spec(bmat.shape),
            dir_spec(cmat.shape),
            dir_spec(prm["abar"].shape),
            dir_spec(prm["tables"][seg].shape),
            pl.BlockSpec((1, wbr), lambda b, dr, k: (0, 0)),
            pl.BlockSpec((wbr, wbr), lambda b, dr, k: (0, 0)),
            pl.BlockSpec((1, wbr), lambda b, dr, k: (0, 0)),
        ],
        out_specs=[
            pl.BlockSpec((1, tc, wbr), lambda b, dr, k: (b, out_chunk(dr, k), 0)),
            pl.BlockSpec((1, 1, 1, sp2), lambda b, dr, k: (b, dr, 0, 0)),
        ],
        out_shape=[
            jax.ShapeDtypeStruct((bsz, n, wbr), BF16),
            jax.ShapeDtypeStruct((bsz, 2, 1, sp2), F32),
        ],
        scratch_shapes=[
            pltpu.VMEM((sp2 // lt, tc, lt), F32),
            pltpu.VMEM((tc, sp2), BF16),
            pltpu.VMEM((n, wbr), F32),
            pltpu.VMEM((N_SEG, sp2), F32),
            pltpu.VMEM((1, sp2), F32),
        ],
        compiler_params=_cparams(("parallel", "arbitrary", "arbitrary")),
        name="s5_branch",
    )(z3, z3, h0, bmat, cmat, prm["abar"], prm["tables"][seg], d.reshape(1, wbr).astype(F32),
      w_glu.astype(BF16), b_glu.reshape(1, wbr).astype(F32))


def _merge_kernel(x_ref, mod_ref, y0, y1, y2, y3, w0, w1, w2, w3, bm_ref, u0, u1, u2, u3, o_ref, h_ref, *, d):
    @pl.when(pl.program_id(1) == 0)
    def _():
        shift = mod_ref[0, :, 0:d]
        scale = mod_ref[0, :, d:2 * d]
        h_ref[...] = (_ln(x_ref[...]) * (1.0 + scale) + shift).astype(BF16)

    h = h_ref[...]
    acc = None
    for i, (y, w, u) in enumerate(zip((y0, y1, y2, y3), (w0, w1, w2, w3), (u0, u1, u2, u3))):
        term = jax.nn.sigmoid(_dot(h, w[...]) + bm_ref[i]) * _dot(y[...], u[0])
        acc = term if acc is None else acc + term
    o_ref[...] = acc.astype(BF16)


def _merge(x2, mod, ys, w_m, b_m, w_up, tm, tn):
    t, d = x2.shape
    wbr = ys[0].shape[1]
    n_mod = mod.shape[0]
    tiles_per_mod = (t // n_mod) // tm
    ns = d // tn
    y_spec = pl.BlockSpec((tm, wbr), lambda i, s: (i, 0))
    w_specs = [pl.BlockSpec((d, tn), functools.partial(lambda i, s, br: (0, br * ns + s), br=br))
               for br in range(N_BRANCH)]
    u_specs = [pl.BlockSpec((1, wbr, tn), functools.partial(lambda i, s, br: (br, 0, s), br=br))
               for br in range(N_BRANCH)]
    return pl.pallas_call(
        functools.partial(_merge_kernel, d=d),
        grid=(t // tm, ns),
        in_specs=[
            pl.BlockSpec((tm, d), lambda i, s: (i, 0)),
            pl.BlockSpec((1, 1, 3 * d), lambda i, s: (i // tiles_per_mod, 0, 0)),
            y_spec, y_spec, y_spec, y_spec,
            *w_specs,
            pl.BlockSpec((N_BRANCH, 1, tn), lambda i, s: (0, 0, s)),
            *u_specs,
        ],
        out_specs=pl.BlockSpec((tm, tn), lambda i, s: (i, s)),
        out_shape=jax.ShapeDtypeStruct((t, d), BF16),
        scratch_shapes=[pltpu.VMEM((tm, d), BF16)],
        compiler_params=_cparams(("parallel", "arbitrary")),
        name="merge",
    )(x2, mod, *ys, w_m, w_m, w_m, w_m, b_m, w_up, w_up, w_up, w_up)


def _outproj_kernel(m_ref, x_ref, mod_ref, w_ref, g_ref, b_ref, o_ref, *, d, alpha):
    gate = mod_ref[0, :, 2 * d:3 * d]
    out = _dot(m_ref[...], w_ref[...])
    o_ref[...] = _ln(alpha * x_ref[...] + gate * out) * g_ref[...] + b_ref[...]


def _out_proj(m2, x2, mod, w_o, ln_g, ln_b, alpha, tm):
    t, d = x2.shape
    n_mod = mod.shape[0]
    tiles_per_mod = (t // n_mod) // tm
    return pl.pallas_call(
        functools.partial(_outproj_kernel, d=d, alpha=alpha),
        grid=(t // tm,),
        in_specs=[
            pl.BlockSpec((tm, d), lambda i: (i, 0)),
            pl.BlockSpec((tm, d), lambda i: (i, 0)),
            pl.BlockSpec((1, 1, 3 * d), lambda i: (i // tiles_per_mod, 0, 0)),
            pl.BlockSpec((d, d), lambda i: (0, 0)),
            pl.BlockSpec((1, d), lambda i: (0, 0)),
            pl.BlockSpec((1, d), lambda i: (0, 0)),
        ],
        out_specs=pl.BlockSpec((tm, d), lambda i: (i, 0)),
        out_shape=jax.ShapeDtypeStruct((t, d), F32),
        compiler_params=_cparams(("parallel",)),
        name="out_proj",
    )(m2, x2, mod, w_o, ln_g.reshape(1, d), ln_b.reshape(1, d))


def kernel(x, c, ctx, c_ctx, w_ada, b_ada, w_in, b_in, w_sgu, b_sgu, w_fnet, b_fnet, rpb, s5_a_re, s5_a_im,
           s5_log_dt, s5_b_re, s5_b_im, s5_c_re, s5_c_im, s5_d, s5_w_glu, s5_b_glu, w_up, w_o, ln_g, ln_b):
    bsz, n, d = x.shape
    lc = ctx.shape[1]
    depth = w_ada.shape[0]
    wbr = d // N_BRANCH
    heads = rpb.shape[1]
    win_h, win_w = (rpb.shape[2] + 1) // 2, (rpb.shape[3] + 1) // 2
    rows = n // GRID_W
    alpha = (2 * depth) ** 0.25

    blk_u, blk_k, blk_v, blk_q, blk_uv, blk_b, blk_gate = 0, 1, 2, 3, 2, 6, 7
    off_merge = (blk_gate + N_BRANCH) * wbr
    ctx_side_cols = 3 * wbr

    n_rows = -(-(bsz + 1) // 8) * 8
    cond = jnp.zeros((n_rows, d), F32).at[:bsz].set(c).at[bsz].set(c_ctx)
    ada = _ada_all(cond, w_ada, b_ada)

    w_a = w_in[:, :, :off_merge].astype(BF16)
    w_m = w_in[:, :, off_merge:].astype(BF16)
    w_up_b = w_up.astype(BF16)
    w_o_b = w_o.astype(BF16)

    plan = _nattn_plan(rows, GRID_W, win_h, win_w)
    gd_f = w_fnet.shape[2]
    cc, sc = _dft_tables(gd_f)
    dft_x = _dft_tables(n) + (cc, sc)
    dft_c = _dft_tables(lc) + (cc, sc)

    tc_x, tc_c = _pick_tile(n, 512), _pick_tile(lc, 512)
    tm_x, tm_c = _pick_tile(n, 512), _pick_tile(bsz * lc, 512)
    tm_in = _pick_tile(n, 1024)
    tn_in = off_merge // 4 if (off_merge // 4) % LANES == 0 else wbr
    sp2 = 2 * s5_a_re.shape[2] * s5_a_re.shape[3]

    xt = x.reshape(bsz * n, d)
    ct = ctx.reshape(bsz * lc, d)
    for l in range(depth):
        need_ctx = l < depth - 1
        mod_x = ada[l, :bsz].reshape(bsz, 1, 3 * d)
        mod_c = ada[l, bsz:bsz + 1].reshape(1, 1, 3 * d)
        b_a = b_in[l, :off_merge].reshape(1, off_merge)
        b_m = b_in[l, off_merge:].reshape(N_BRANCH, 1, d)

        zx = _in_proj(xt, mod_x, w_a[l], b_a, off_merge, tm_in, tn_in).reshape(bsz, n, off_merge)
        c_cols = off_merge if need_ctx else ctx_side_cols
        zc = _in_proj(ct, mod_c, w_a[l], b_a, c_cols, tm_c, tn_in if need_ctx else wbr).reshape(bsz, lc, c_cols)

        prm = _s5_params(s5_a_re[l], s5_a_im[l], s5_log_dt[l], s5_b_re[l], s5_b_im[l], s5_c_re[l], s5_c_im[l],
                         {tc_x // N_SEG, tc_c // N_SEG})
        gate_c = blk_gate + 3 if need_ctx else blk_u
        yd_c, h_ctx = _s5(zc, jnp.zeros((bsz, 2, 1, sp2), F32), prm, s5_d[l], s5_w_glu[l], s5_b_glu[l],
                          wbr, blk_u, gate_c, tc_c)
        yd_x, _ = _s5(zx, h_ctx, prm, s5_d[l], s5_w_glu[l], s5_b_glu[l], wbr, blk_u, blk_gate + 3, tc_x)

        bias = _nattn_bias(rpb[l], plan, GRID_W)
        yc_x = _nattn(zx, zc, bias, plan, heads, wbr, (blk_k, blk_v, blk_q), blk_gate + 2)
        ya_x = _sgu(zx, w_sgu[l], b_sgu[l], wbr, blk_uv, blk_gate + 0, _pick_tile(n, 512))
        yb_x = _fnet(zx, w_fnet[l], b_fnet[l], dft_x, wbr, blk_b, blk_gate + 1, _pick_tile(n, 512))

        ys = [y.reshape(bsz * n, wbr) for y in (ya_x, yb_x, yc_x, yd_x)]
        m = _merge(xt, mod_x, ys, w_m[l], b_m, w_up_b[l], tm_x, _pick_tile(d, 512))
        xt_new = _out_proj(m, xt, mod_x, w_o_b[l], ln_g[l], ln_b[l], alpha, tm_x)

        if need_ctx:
            yc_c = _cattn(zc, heads, wbr, (blk_k, blk_v, blk_q), blk_gate + 2)
            ya_c = _sgu(zc, w_sgu[l], b_sgu[l], wbr, blk_uv, blk_gate + 0, _pick_tile(lc, 512))
            yb_c = _fnet(zc, w_fnet[l], b_fnet[l], dft_c, wbr, blk_b, blk_gate + 1, _pick_tile(lc, 512))
            ys_c = [y.reshape(bsz * lc, wbr) for y in (ya_c, yb_c, yc_c, yd_c)]
            m_c = _merge(ct, mod_c, ys_c, w_m[l], b_m, w_up_b[l], tm_c, _pick_tile(d, 512))
            ct = _out_proj(m_c, ct, mod_c, w_o_b[l], ln_g[l], ln_b[l], alpha, tm_c)
        xt = xt_new
    return xt.reshape(bsz, n, d)
```

```python
import functools
import math

import jax
import jax.numpy as jnp
import numpy as np
from jax import lax
from jax.experimental import pallas as pl
from jax.experimental.pallas import tpu as pltpu

GRID_W = 64
N_BRANCH = 4
LN_EPS = 1e-6
NEG_INF = -1e30
Q_ROWS = 4
SUBLANES = 8
LANES = 128
S5_ROWS = 64
VMEM_LIMIT = 56 * 1024 * 1024

F32 = jnp.float32
BF16 = jnp.bfloat16


def _cparams(sem):
    return pltpu.CompilerParams(dimension_semantics=sem, vmem_limit_bytes=VMEM_LIMIT)


def _dot(a, b):
    return jnp.dot(a, b, preferred_element_type=F32)


def _dot_t(a, b):
    return lax.dot_general(a, b, (((1,), (1,)), ((), ())), preferred_element_type=F32)


def _ln(x):
    mu = jnp.mean(x, axis=-1, keepdims=True)
    xc = x - mu
    var = jnp.mean(xc * xc, axis=-1, keepdims=True)
    return xc * lax.rsqrt(var + LN_EPS)


def _silu(x):
    return x * jax.nn.sigmoid(x)


def _pick_tile(n, target):
    t = min(n, target)
    while n % t:
        t //= 2
    return t


def _ada_kernel(s_ref, w_ref, b_ref, o_ref):
    s = _silu(s_ref[...]).astype(BF16)
    o_ref[0] = _dot(s, w_ref[0].astype(BF16)) + b_ref[0]


def _ada_all(cond, w_ada, b_ada):
    depth, d, d3 = w_ada.shape
    r = cond.shape[0]
    tn = _pick_tile(d3, 1024)
    return pl.pallas_call(
        _ada_kernel,
        grid=(depth, d3 // tn),
        in_specs=[
            pl.BlockSpec((r, d), lambda l, j: (0, 0)),
            pl.BlockSpec((1, d, tn), lambda l, j: (l, 0, j)),
            pl.BlockSpec((1, 1, tn), lambda l, j: (l, 0, j)),
        ],
        out_specs=pl.BlockSpec((1, r, tn), lambda l, j: (l, 0, j)),
        out_shape=jax.ShapeDtypeStruct((depth, r, d3), F32),
        compiler_params=_cparams(("arbitrary", "arbitrary")),
        name="ada_proj",
    )(cond, w_ada, b_ada.reshape(depth, 1, d3))


def _inproj_kernel(x_ref, mod_ref, w_ref, b_ref, o_ref, h_ref, *, d):
    @pl.when(pl.program_id(1) == 0)
    def _():
        shift = mod_ref[0, :, 0:d]
        scale = mod_ref[0, :, d:2 * d]
        h_ref[...] = (_ln(x_ref[...]) * (1.0 + scale) + shift).astype(BF16)

    o_ref[...] = _dot(h_ref[...], w_ref[...]) + b_ref[...]


def _in_proj(x2, mod, w, b, ncols, tm, tn):
    t, d = x2.shape
    n_mod = mod.shape[0]
    tiles_per_mod = (t // n_mod) // tm
    return pl.pallas_call(
        functools.partial(_inproj_kernel, d=d),
        grid=(t // tm, ncols // tn),
        in_specs=[
            pl.BlockSpec((tm, d), lambda i, j: (i, 0)),
            pl.BlockSpec((1, 1, 3 * d), lambda i, j: (i // tiles_per_mod, 0, 0)),
            pl.BlockSpec((d, tn), lambda i, j: (0, j)),
            pl.BlockSpec((1, tn), lambda i, j: (0, j)),
        ],
        out_specs=pl.BlockSpec((tm, tn), lambda i, j: (i, j)),
        out_shape=jax.ShapeDtypeStruct((t, ncols), F32),
        scratch_shapes=[pltpu.VMEM((tm, d), BF16)],
        compiler_params=_cparams(("parallel", "arbitrary")),
        name="in_proj",
    )(x2, mod, w, b)


def _sgu_kernel(uv_ref, g_ref, w_ref, b_ref, o_ref, *, wbr, chunk, groups):
    uv = jax.nn.gelu(uv_ref[0])
    u = uv[:, :wbr]
    v = _ln(uv[:, wbr:]).astype(BF16)
    gate = _silu(g_ref[0])
    gd = wbr // groups
    for k in range(uv.shape[0] // chunk):
        rows = slice(k * chunk, (k + 1) * chunk)
        for g in range(groups):
            cols = slice(g * gd, (g + 1) * gd)
            mixed = _dot(w_ref[g], v[rows, cols]) + b_ref[g]
            o_ref[0, rows, cols] = (u[rows, cols] * mixed * gate[rows, cols]).astype(BF16)


def _sgu(z3, w_sgu, b_sgu, wbr, uv_blk, gate_blk, ta):
    bsz, n, _ = z3.shape
    groups, chunk, _ = w_sgu.shape
    return pl.pallas_call(
        functools.partial(_sgu_kernel, wbr=wbr, chunk=chunk, groups=groups),
        grid=(bsz, n // ta),
        in_specs=[
            pl.BlockSpec((1, ta, 2 * wbr), lambda b, i: (b, i, uv_blk)),
            pl.BlockSpec((1, ta, wbr), lambda b, i: (b, i, gate_blk)),
            pl.BlockSpec((groups, chunk, chunk), lambda b, i: (0, 0, 0)),
            pl.BlockSpec((groups, chunk, 1), lambda b, i: (0, 0, 0)),
        ],
        out_specs=pl.BlockSpec((1, ta, wbr), lambda b, i: (b, i, 0)),
        out_shape=jax.ShapeDtypeStruct((bsz, n, wbr), BF16),
        compiler_params=_cparams(("parallel", "arbitrary")),
        name="sgu_branch",
    )(z3, z3, w_sgu.astype(BF16), b_sgu.reshape(groups, chunk, 1))


def _dft_tables(n):
    k = jnp.arange(n, dtype=jnp.int32)
    r = (k[:, None] * k[None, :]) % n
    ang = r.astype(F32) * (2.0 * math.pi / n)
    return jnp.cos(ang).astype(BF16), jnp.sin(ang).astype(BF16)


def _fnet_kernel(z_ref, g_ref, cc_ref, sc_ref, cn_ref, sn_ref, wf_ref, bf_ref, o_ref, p_ref, q_ref,
                 *, groups, norm):
    wbr = z_ref.shape[2]
    gd = wbr // groups

    @pl.when(pl.program_id(1) == 0)
    def _():
        for g in range(groups):
            cols = slice(g * gd, (g + 1) * gd)
            zg = z_ref[0, :, cols].astype(BF16)
            p_ref[:, cols] = _dot(zg, cc_ref[...]).astype(BF16)
            q_ref[:, cols] = _dot(zg, sc_ref[...]).astype(BF16)

    f = (_dot(cn_ref[...], p_ref[...]) - _dot(sn_ref[...], q_ref[...])) * norm
    gate = _silu(g_ref[0])
    for g in range(groups):
        cols = slice(g * gd, (g + 1) * gd)
        y = _dot(f[:, cols].astype(BF16), wf_ref[g]) + bf_ref[:, cols]
        o_ref[0, :, cols] = (y * gate[:, cols]).astype(BF16)


def _fnet(z3, w_f, b_f, tables, wbr, z_blk, gate_blk, tk):
    bsz, n, _ = z3.shape
    groups, gd, _ = w_f.shape
    cn, sn, cc, sc = tables
    return pl.pallas_call(
        functools.partial(_fnet_kernel, groups=groups, norm=1.0 / math.sqrt(n * gd)),
        grid=(bsz, n // tk),
        in_specs=[
            pl.BlockSpec((1, n, wbr), lambda b, k: (b, 0, z_blk)),
            pl.BlockSpec((1, tk, wbr), lambda b, k: (b, k, gate_blk)),
            pl.BlockSpec((gd, gd), lambda b, k: (0, 0)),
            pl.BlockSpec((gd, gd), lambda b, k: (0, 0)),
            pl.BlockSpec((tk, n), lambda b, k: (k, 0)),
            pl.BlockSpec((tk, n), lambda b, k: (k, 0)),
            pl.BlockSpec((groups, gd, gd), lambda b, k: (0, 0, 0)),
            pl.BlockSpec((1, wbr), lambda b, k: (0, 0)),
        ],
        out_specs=pl.BlockSpec((1, tk, wbr), lambda b, k: (b, k, 0)),
        out_shape=jax.ShapeDtypeStruct((bsz, n, wbr), BF16),
        scratch_shapes=[pltpu.VMEM((n, wbr), BF16), pltpu.VMEM((n, wbr), BF16)],
        compiler_params=_cparams(("parallel", "arbitrary")),
        name="fnet_branch",
    )(z3, z3, cc, sc, cn, sn, w_f.astype(BF16), b_f.reshape(1, wbr))


def _nattn_plan(rows, width, win_h, win_w):
    kh, kw = min(win_h, rows), min(win_w, width)
    qb = Q_ROWS
    wr = min(kh + qb - 1, rows)
    nblk = rows // qb
    r0 = np.arange(nblk) * qb
    ks = np.clip(r0 - kh // 2, 0, rows - wr)
    r = r0[:, None] + np.arange(qb)[None]
    rs = np.clip(r - kh // 2, 0, rows - kh)
    kr = ks[:, None] + np.arange(wr)[None]
    row_ok = (kr[:, None, :] >= rs[:, :, None]) & (kr[:, None, :] < rs[:, :, None] + kh)
    assert row_ok.sum(-1).min() == kh
    dri = np.where(row_ok, kr[:, None, :] - r[:, :, None] + win_h - 1, 0)
    key = np.concatenate([dri.reshape(nblk, -1), row_ok.reshape(nblk, -1)], axis=1)
    _, first, inv = np.unique(key, axis=0, return_index=True, return_inverse=True)
    qc = np.arange(width)
    col_start = np.clip(qc - kw // 2, 0, width - kw)
    in_win = (qc[None, :] >= col_start[:, None]) & (qc[None, :] < col_start[:, None] + kw)
    dci = np.clip(qc[None, :] - qc[:, None], -(kw - 1), kw - 1) + win_w - 1
    return dict(qb=qb, wr=wr, nblk=nblk, ks=ks.astype(np.int32), pat=inv.reshape(-1).astype(np.int32),
                dri=dri[first], row_ok=row_ok[first], in_win=in_win, dci=dci)


def _nattn_bias(rpb, plan, width):
    dri, row_ok = plan["dri"], plan["row_ok"]
    npat, qb, wr = dri.shape
    heads, _, n_dc = rpb.shape
    rows = jnp.take(rpb.astype(F32), jnp.asarray(dri.reshape(-1)), axis=1)
    onehot = (plan["dci"].reshape(-1)[None, :] == np.arange(n_dc)[:, None]).astype(np.float32)
    vals = jnp.einsum("hrc,cw->hrw", rows, jnp.asarray(onehot), precision=lax.Precision.HIGHEST)
    vals = vals.reshape(heads, npat, qb, wr, width, width).transpose(1, 0, 2, 4, 3, 5)
    valid = row_ok[:, None, :, None, :, None] & plan["in_win"][None, None, None, :, None, :]
    vals = jnp.where(jnp.asarray(valid), vals, NEG_INF)
    return vals.reshape(npat, heads, qb * width, wr * width)


def _softmax_pv(s_band, s_ctx, v_band, v_ctx):
    m = jnp.maximum(jnp.max(s_band, axis=-1, keepdims=True), jnp.max(s_ctx, axis=-1, keepdims=True))
    p_band = jnp.exp(s_band - m)
    p_ctx = jnp.exp(s_ctx - m)
    den = jnp.sum(p_band, axis=-1, keepdims=True) + jnp.sum(p_ctx, axis=-1, keepdims=True)
    out = _dot(p_band.astype(BF16), v_band) + _dot(p_ctx.astype(BF16), v_ctx)
    return out / den


def _nattn_kernel(ks_ref, pat_ref, q_ref, k_ref, v_ref, ck_ref, cv_ref, g_ref, bias_ref, o_ref,
                  *, heads, width, nkeys):
    blk = pl.program_id(1)
    start = pl.multiple_of(ks_ref[blk] * width, width)
    pat = pat_ref[blk]
    dh = q_ref.shape[2] // heads
    scale = dh ** -0.5
    gate = _silu(g_ref[0])
    for h in range(heads):
        cols = slice(h * dh, (h + 1) * dh)
        q = q_ref[0, :, cols].astype(BF16)
        kb = k_ref[0, pl.ds(start, nkeys), cols].astype(BF16)
        vb = v_ref[0, pl.ds(start, nkeys), cols].astype(BF16)
        s_band = _dot_t(q, kb) * scale + bias_ref[pat, h]
        s_ctx = _dot_t(q, ck_ref[0, :, cols].astype(BF16)) * scale
        out = _softmax_pv(s_band, s_ctx, vb, cv_ref[0, :, cols].astype(BF16))
        o_ref[0, :, cols] = (out * gate[:, cols]).astype(BF16)


def _nattn(zx3, zc3, bias, plan, heads, wbr, blks, gate_blk):
    bsz, n, _ = zx3.shape
    lc = zc3.shape[1]
    k_blk, v_blk, q_blk = blks
    nq = plan["qb"] * GRID_W
    nkeys = plan["wr"] * GRID_W
    grid_spec = pltpu.PrefetchScalarGridSpec(
        num_scalar_prefetch=2,
        grid=(bsz, plan["nblk"]),
        in_specs=[
            pl.BlockSpec((1, nq, wbr), lambda b, i, *_: (b, i, q_blk)),
            pl.BlockSpec((1, n, wbr), lambda b, i, *_: (b, 0, k_blk)),
            pl.BlockSpec((1, n, wbr), lambda b, i, *_: (b, 0, v_blk)),
            pl.BlockSpec((1, lc, wbr), lambda b, i, *_: (b, 0, k_blk)),
            pl.BlockSpec((1, lc, wbr), lambda b, i, *_: (b, 0, v_blk)),
            pl.BlockSpec((1, nq, wbr), lambda b, i, *_: (b, i, gate_blk)),
            pl.BlockSpec(bias.shape, lambda b, i, *_: (0, 0, 0, 0)),
        ],
        out_specs=pl.BlockSpec((1, nq, wbr), lambda b, i, *_: (b, i, 0)),
    )
    return pl.pallas_call(
        functools.partial(_nattn_kernel, heads=heads, width=GRID_W, nkeys=nkeys),
        grid_spec=grid_spec,
        out_shape=jax.ShapeDtypeStruct((bsz, n, wbr), BF16),
        compiler_params=_cparams(("parallel", "arbitrary")),
        name="nattn_branch",
    )(jnp.asarray(plan["ks"]), jnp.asarray(plan["pat"]), zx3, zx3, zx3, zc3, zc3, zx3, bias)


def _cattn_kernel(q_ref, k_ref, v_ref, g_ref, o_ref, *, heads):
    dh = q_ref.shape[2] // heads
    scale = dh ** -0.5
    gate = _silu(g_ref[0])
    for h in range(heads):
        cols = slice(h * dh, (h + 1) * dh)
        s = _dot_t(q_ref[0, :, cols].astype(BF16), k_ref[0, :, cols].astype(BF16)) * scale
        p = jnp.exp(s - jnp.max(s, axis=-1, keepdims=True))
        out = _dot(p.astype(BF16), v_ref[0, :, cols].astype(BF16)) / jnp.sum(p, axis=-1, keepdims=True)
        o_ref[0, :, cols] = (out * gate[:, cols]).astype(BF16)


def _cattn(zc3, heads, wbr, blks, gate_blk):
    bsz, lc, _ = zc3.shape
    k_blk, v_blk, q_blk = blks
    spec = lambda blk: pl.BlockSpec((1, lc, wbr), lambda b: (b, 0, blk))
    return pl.pallas_call(
        functools.partial(_cattn_kernel, heads=heads),
        grid=(bsz,),
        in_specs=[spec(q_blk), spec(k_blk), spec(v_blk), spec(gate_blk)],
        out_specs=spec(0),
        out_shape=jax.ShapeDtypeStruct((bsz, lc, wbr), BF16),
        compiler_params=_cparams(("parallel",)),
        name="cattn_branch",
    )(zc3, zc3, zc3, zc3)


def _s5_discretise(a_re, a_im, log_dt, b_re, b_im):
    dt = jnp.exp(log_dt)[..., None]
    lr, li = a_re * dt, a_im * dt
    mag = jnp.exp(lr)
    abar_re, abar_im = mag * jnp.cos(li), mag * jnp.sin(li)
    den = a_re * a_re + a_im * a_im
    f_re = ((abar_re - 1.0) * a_re + abar_im * a_im) / den
    f_im = (abar_im * a_re - (abar_re - 1.0) * a_im) / den
    bb_re = f_re[..., None] * b_re - f_im[..., None] * b_im
    bb_im = f_re[..., None] * b_im + f_im[..., None] * b_re
    return abar_re, abar_im, bb_re, bb_im


def _block_diag(m, gpb):
    two, g, a, b = m.shape
    nb = g // gpb
    m = m.reshape(two, nb, gpb, a, b)
    eye = jnp.eye(gpb, dtype=m.dtype)
    out = m[:, :, :, :, None, :] * eye[None, None, :, None, :, None]
    return out.reshape(two, nb, gpb * a, gpb * b)


def _s5_params(a_re, a_im, log_dt, b_re, b_im, c_re, c_im):
    f32 = lambda t: t.astype(F32)
    a_re, a_im, log_dt, b_re, b_im, c_re, c_im = map(f32, (a_re, a_im, log_dt, b_re, b_im, c_re, c_im))
    _, g, p, h = b_re.shape
    gpb = min(g, max(1, LANES // h))
    abar_re, abar_im, bb_re, bb_im = _s5_discretise(a_re, a_im, log_dt, b_re, b_im)
    tr = lambda t: jnp.swapaxes(t, -1, -2)
    bmat = jnp.concatenate([_block_diag(tr(bb_re), gpb), _block_diag(tr(bb_im), gpb)], axis=-1)
    cmat = jnp.stack([_block_diag(tr(c_re), gpb), _block_diag(-tr(c_im), gpb)], axis=2)
    abar = jnp.concatenate([abar_re.reshape(2, 1, g * p), abar_im.reshape(2, 1, g * p)], axis=-1)
    return dict(bmat=bmat.astype(BF16), cmat=cmat.astype(BF16), abar=abar)


def _interleave_perm(nseq, rows):
    src = np.arange(nseq * rows)
    dst = (src % rows) * nseq + src // rows
    pm = np.zeros((nseq * rows, nseq * rows), np.float32)
    pm[dst, src] = 1.0
    return pm


def _cmul_add(ar, ai, xr, xi, br, bi):
    return ar * xr - ai * xi + br, ar * xi + ai * xr + bi


def _permute_rows_exact(pm, y):
    hi = y.astype(BF16)
    r1 = y - hi.astype(F32)
    mid = r1.astype(BF16)
    lo = (r1 - mid.astype(F32)).astype(BF16)
    return _dot(pm, hi) + _dot(pm, mid) + _dot(pm, lo)


def _s5_kernel(*refs, reverse, final, tiles_per_blk):
    if final:
        (u_ref, h0_ref, pm_ref, pmt_ref, bm_ref, cm_ref, a_ref, yf_ref, g_ref, d_ref, wg_ref, bg_ref,
         o_ref, ht_ref, xs_ref, xb_ref, st_ref) = refs
    else:
        (u_ref, h0_ref, pm_ref, pmt_ref, bm_ref, cm_ref, a_ref,
         o_ref, ht_ref, xs_ref, xb_ref, st_ref) = refs
    nseq, tcb, wbr = u_ref.shape
    n_tiles, lt = xs_ref.shape[0] // 2, xs_ref.shape[2]
    sp = n_tiles * lt
    nb = bm_ref.shape[1]
    bw_in, bw_st = wbr // nb, sp // nb
    tpb = bw_st // lt
    re_cols = lambda t: slice(t * lt, (t + 1) * lt)
    im_cols = lambda t: slice(sp + t * lt, sp + (t + 1) * lt)

    @pl.when(pl.program_id(1) == 0)
    def _():
        st_ref[...] = h0_ref[...]

    u = u_ref[...].reshape(nseq * tcb, wbr)
    ub = _dot(pm_ref[...], u.astype(BF16)).astype(BF16)
    for n in range(nb):
        bu = _dot(ub[:, n * bw_in:(n + 1) * bw_in], bm_ref[0, n])
        for t in range(tpb):
            xs_ref[n * tpb + t] = bu[:, t * lt:(t + 1) * lt]
            xs_ref[n_tiles + n * tpb + t] = bu[:, bw_st + t * lt:bw_st + (t + 1) * lt]

    npairs = tcb // 2
    for q in range(n_tiles // tiles_per_blk):
        tiles = range(q * tiles_per_blk, (q + 1) * tiles_per_blk)
        ar = [jnp.broadcast_to(a_ref[0, :, re_cols(t)], (nseq, lt)) for t in tiles]
        ai = [jnp.broadcast_to(a_ref[0, :, im_cols(t)], (nseq, lt)) for t in tiles]

        def step(i, st):
            jj = npairs - 1 - i if reverse else i
            r0 = pl.multiple_of(jj * 2 * nseq, 2 * nseq)
            out = []
            for idx, t in enumerate(tiles):
                dr, di = xs_ref[t, pl.ds(r0, 2 * nseq), :], xs_ref[n_tiles + t, pl.ds(r0, 2 * nseq), :]
                first, second = (slice(nseq, None), slice(0, nseq)) if reverse else (slice(0, nseq), slice(nseq, None))
                x0 = _cmul_add(ar[idx], ai[idx], st[2 * idx], st[2 * idx + 1], dr[first], di[first])
                x1 = _cmul_add(ar[idx], ai[idx], x0[0], x0[1], dr[second], di[second])
                lo, hi = (x1, x0) if reverse else (x0, x1)
                xb_ref[pl.ds(r0, 2 * nseq), re_cols(t)] = jnp.concatenate([lo[0], hi[0]], axis=0).astype(BF16)
                xb_ref[pl.ds(r0, 2 * nseq), im_cols(t)] = jnp.concatenate([lo[1], hi[1]], axis=0).astype(BF16)
                out += [x1[0], x1[1]]
            return tuple(out)

        init = []
        for t in tiles:
            init += [st_ref[:, re_cols(t)], st_ref[:, im_cols(t)]]
        last = lax.fori_loop(0, npairs, step, tuple(init))
        for idx, t in enumerate(tiles):
            st_ref[:, re_cols(t)] = last[2 * idx]
            st_ref[:, im_cols(t)] = last[2 * idx + 1]

    ht_ref[...] = st_ref[...]

    parts = []
    for n in range(nb):
        parts.append(_dot(xb_ref[:, n * bw_st:(n + 1) * bw_st], cm_ref[0, n, 0])
                     + _dot(xb_ref[:, sp + n * bw_st:sp + (n + 1) * bw_st], cm_ref[0, n, 1]))
    y = parts[0] if nb == 1 else jnp.concatenate(parts, axis=-1)
    y = _permute_rows_exact(pmt_ref[...], y)

    if final:
        tot = jax.nn.gelu(yf_ref[...].reshape(nseq * tcb, wbr) + y + d_ref[...] * u)
        glu = jax.nn.sigmoid(_dot(tot.astype(BF16), wg_ref[...]) + bg_ref[...])
        gate = _silu(g_ref[...].reshape(nseq * tcb, wbr))
        o_ref[...] = (tot * glu * gate).astype(BF16).reshape(nseq, tcb, wbr)
    else:
        o_ref[...] = y.reshape(nseq, tcb, wbr)


def _s5_pass(z3, h0, prm, dirn, wbr, u_blk, tcb, final_args=None):
    bsz, n, _ = z3.shape
    assert bsz % SUBLANES == 0 and n % tcb == 0 and tcb % 2 == 0
    nchunks = n // tcb
    reverse = dirn == 1
    final = final_args is not None
    sp2 = prm["abar"].shape[-1]
    bmat, cmat = prm["bmat"], prm["cmat"]
    sp = sp2 // 2
    lt = min(LANES, sp // bmat.shape[1])
    tiles_per_blk = min(sp // lt, 4)
    rows = SUBLANES * tcb
    pm = _interleave_perm(SUBLANES, tcb)
    chunk = (lambda k: nchunks - 1 - k) if reverse else (lambda k: k)
    seq_spec = lambda blk: pl.BlockSpec((SUBLANES, tcb, wbr), lambda g, k: (g, chunk(k), blk))
    const = lambda shape: pl.BlockSpec(shape, lambda g, k: (0,) * len(shape))
    dir_spec = lambda shape: pl.BlockSpec((1,) + shape[1:], lambda g, k: (dirn,) + (0,) * (len(shape) - 1))
    in_specs = [
        seq_spec(u_blk),
        pl.BlockSpec((SUBLANES, sp2), lambda g, k: (g, 0)),
        const((rows, rows)), const((rows, rows)),
        dir_spec(bmat.shape), dir_spec(cmat.shape), dir_spec(prm["abar"].shape),
    ]
    args = [z3, h0, jnp.asarray(pm, BF16), jnp.asarray(pm.T, BF16), bmat, cmat, prm["abar"]]
    if final:
        yf, gate_blk, d, w_glu, b_glu = final_args
        in_specs += [seq_spec(0), seq_spec(gate_blk), const((1, wbr)), const((wbr, wbr)), const((1, wbr))]
        args += [yf, z3, d.reshape(1, wbr).astype(F32), w_glu.astype(BF16), b_glu.reshape(1, wbr).astype(F32)]
    return pl.pallas_call(
        functools.partial(_s5_kernel, reverse=reverse, final=final, tiles_per_blk=tiles_per_blk),
        grid=(bsz // SUBLANES, nchunks),
        in_specs=in_specs,
        out_specs=[seq_spec(0), pl.BlockSpec((SUBLANES, sp2), lambda g, k: (g, 0))],
        out_shape=[
            jax.ShapeDtypeStruct((bsz, n, wbr), BF16 if final else F32),
            jax.ShapeDtypeStruct((bsz, sp2), F32),
        ],
        scratch_shapes=[
            pltpu.VMEM((sp2 // lt, rows, lt), F32),
            pltpu.VMEM((rows, sp2), BF16),
            pltpu.VMEM((SUBLANES, sp2), F32),
        ],
        compiler_params=_cparams(("parallel", "arbitrary")),
        name="s5_final" if final else "s5_fwd",
    )(*args)


def _s5(z3, h0, prm, d, w_glu, b_glu, wbr, u_blk, gate_blk, tcb):
    yf, hf = _s5_pass(z3, h0[0], prm, 0, wbr, u_blk, tcb)
    out, hb = _s5_pass(z3, h0[1], prm, 1, wbr, u_blk, tcb, final_args=(yf, gate_blk, d, w_glu, b_glu))
    return out, jnp.stack([hf, hb])


def _merge_kernel(x_ref, mod_ref, y0, y1, y2, y3, w0, w1, w2, w3, bm_ref, u0, u1, u2, u3, o_ref, h_ref, *, d):
    @pl.when(pl.program_id(1) == 0)
    def _():
        shift = mod_ref[0, :, 0:d]
        scale = mod_ref[0, :, d:2 * d]
        h_ref[...] = (_ln(x_ref[...]) * (1.0 + scale) + shift).astype(BF16)

    h = h_ref[...]
    acc = None
    for i, (y, w, u) in enumerate(zip((y0, y1, y2, y3), (w0, w1, w2, w3), (u0, u1, u2, u3))):
        term = jax.nn.sigmoid(_dot(h, w[...]) + bm_ref[i]) * _dot(y[...], u[0])
        acc = term if acc is None else acc + term
    o_ref[...] = acc.astype(BF16)


def _merge(x2, mod, ys, w_m, b_m, w_up, tm, tn):
    t, d = x2.shape
    wbr = ys[0].shape[1]
    n_mod = mod.shape[0]
    tiles_per_mod = (t // n_mod) // tm
    ns = d // tn
    y_spec = pl.BlockSpec((tm, wbr), lambda i, s: (i, 0))
    w_specs = [pl.BlockSpec((d, tn), functools.partial(lambda i, s, br: (0, br * ns + s), br=br))
               for br in range(N_BRANCH)]
    u_specs = [pl.BlockSpec((1, wbr, tn), functools.partial(lambda i, s, br: (br, 0, s), br=br))
               for br in range(N_BRANCH)]
    return pl.pallas_call(
        functools.partial(_merge_kernel, d=d),
        grid=(t // tm, ns),
        in_specs=[
            pl.BlockSpec((tm, d), lambda i, s: (i, 0)),
            pl.BlockSpec((1, 1, 3 * d), lambda i, s: (i // tiles_per_mod, 0, 0)),
            y_spec, y_spec, y_spec, y_spec,
            *w_specs,
            pl.BlockSpec((N_BRANCH, 1, tn), lambda i, s: (0, 0, s)),
            *u_specs,
        ],
        out_specs=pl.BlockSpec((tm, tn), lambda i, s: (i, s)),
        out_shape=jax.ShapeDtypeStruct((t, d), BF16),
        scratch_shapes=[pltpu.VMEM((tm, d), BF16)],
        compiler_params=_cparams(("parallel", "arbitrary")),
        name="merge",
    )(x2, mod, *ys, w_m, w_m, w_m, w_m, b_m, w_up, w_up, w_up, w_up)


def _outproj_kernel(m_ref, x_ref, mod_ref, w_ref, g_ref, b_ref, o_ref, *, d, alpha):
    gate = mod_ref[0, :, 2 * d:3 * d]
    out = _dot(m_ref[...], w_ref[...])
    o_ref[...] = _ln(alpha * x_ref[...] + gate * out) * g_ref[...] + b_ref[...]


def _out_proj(m2, x2, mod, w_o, ln_g, ln_b, alpha, tm):
    t, d = x2.shape
    n_mod = mod.shape[0]
    tiles_per_mod = (t // n_mod) // tm
    return pl.pallas_call(
        functools.partial(_outproj_kernel, d=d, alpha=alpha),
        grid=(t // tm,),
        in_specs=[
            pl.BlockSpec((tm, d), lambda i: (i, 0)),
            pl.BlockSpec((tm, d), lambda i: (i, 0)),
            pl.BlockSpec((1, 1, 3 * d), lambda i: (i // tiles_per_mod, 0, 0)),
            pl.BlockSpec((d, d), lambda i: (0, 0)),
            pl.BlockSpec((1, d), lambda i: (0, 0)),
            pl.BlockSpec((1, d), lambda i: (0, 0)),
        ],
        out_specs=pl.BlockSpec((tm, d), lambda i: (i, 0)),
        out_shape=jax.ShapeDtypeStruct((t, d), F32),
        compiler_params=_cparams(("parallel",)),
        name="out_proj",
    )(m2, x2, mod, w_o, ln_g.reshape(1, d), ln_b.reshape(1, d))


def kernel(x, c, ctx, c_ctx, w_ada, b_ada, w_in, b_in, w_sgu, b_sgu, w_fnet, b_fnet, rpb, s5_a_re, s5_a_im,
           s5_log_dt, s5_b_re, s5_b_im, s5_c_re, s5_c_im, s5_d, s5_w_glu, s5_b_glu, w_up, w_o, ln_g, ln_b):
    bsz, n, d = x.shape
    lc = ctx.shape[1]
    depth = w_ada.shape[0]
    wbr = d // N_BRANCH
    heads = rpb.shape[1]
    win_h, win_w = (rpb.shape[2] + 1) // 2, (rpb.shape[3] + 1) // 2
    rows = n // GRID_W
    alpha = (2 * depth) ** 0.25

    blk_u, blk_k, blk_v, blk_q, blk_uv, blk_b, blk_gate = 0, 1, 2, 3, 2, 6, 7
    off_merge = (blk_gate + N_BRANCH) * wbr
    ctx_side_cols = 3 * wbr

    n_rows = -(-(bsz + 1) // 8) * 8
    cond = jnp.zeros((n_rows, d), F32).at[:bsz].set(c).at[bsz].set(c_ctx)
    ada = _ada_all(cond, w_ada, b_ada)

    w_a = w_in[:, :, :off_merge].astype(BF16)
    w_m = w_in[:, :, off_merge:].astype(BF16)
    w_up_b = w_up.astype(BF16)
    w_o_b = w_o.astype(BF16)

    plan = _nattn_plan(rows, GRID_W, win_h, win_w)
    gd_f = w_fnet.shape[2]
    cc, sc = _dft_tables(gd_f)
    dft_x = _dft_tables(n) + (cc, sc)
    dft_c = _dft_tables(lc) + (cc, sc)

    tcb_x, tcb_c = _pick_tile(n, S5_ROWS), _pick_tile(lc, S5_ROWS)
    tm_x, tm_c = _pick_tile(n, 512), _pick_tile(bsz * lc, 512)
    tm_in = _pick_tile(n, 1024)
    tn_in = off_merge // 4 if (off_merge // 4) % LANES == 0 else wbr
    sp2 = 2 * s5_a_re.shape[2] * s5_a_re.shape[3]

    xt = x.reshape(bsz * n, d)
    ct = ctx.reshape(bsz * lc, d)
    for l in range(depth):
        need_ctx = l < depth - 1
        mod_x = ada[l, :bsz].reshape(bsz, 1, 3 * d)
        mod_c = ada[l, bsz:bsz + 1].reshape(1, 1, 3 * d)
        b_a = b_in[l, :off_merge].reshape(1, off_merge)
        b_m = b_in[l, off_merge:].reshape(N_BRANCH, 1, d)

        zx = _in_proj(xt, mod_x, w_a[l], b_a, off_merge, tm_in, tn_in).reshape(bsz, n, off_merge)
        c_cols = off_merge if need_ctx else ctx_side_cols
        zc = _in_proj(ct, mod_c, w_a[l], b_a, c_cols, tm_c, tn_in if need_ctx else wbr).reshape(bsz, lc, c_cols)

        prm = _s5_params(s5_a_re[l], s5_a_im[l], s5_log_dt[l], s5_b_re[l], s5_b_im[l], s5_c_re[l], s5_c_im[l])
        gate_c = blk_gate + 3 if need_ctx else blk_u
        yd_c, h_ctx = _s5(zc, jnp.zeros((2, bsz, sp2), F32), prm, s5_d[l], s5_w_glu[l], s5_b_glu[l],
                          wbr, blk_u, gate_c, tcb_c)
        yd_x, _ = _s5(zx, h_ctx, prm, s5_d[l], s5_w_glu[l], s5_b_glu[l], wbr, blk_u, blk_gate + 3, tcb_x)

        bias = _nattn_bias(rpb[l], plan, GRID_W)
        yc_x = _nattn(zx, zc, bias, plan, heads, wbr, (blk_k, blk_v, blk_q), blk_gate + 2)
        ya_x = _sgu(zx, w_sgu[l], b_sgu[l], wbr, blk_uv, blk_gate + 0, _pick_tile(n, 512))
        yb_x = _fnet(zx, w_fnet[l], b_fnet[l], dft_x, wbr, blk_b, blk_gate + 1, _pick_tile(n, 512))

        ys = [y.reshape(bsz * n, wbr) for y in (ya_x, yb_x, yc_x, yd_x)]
        m = _merge(xt, mod_x, ys, w_m[l], b_m, w_up_b[l], tm_x, _pick_tile(d, 512))
        xt_new = _out_proj(m, xt, mod_x, w_o_b[l], ln_g[l], ln_b[l], alpha, tm_x)

        if need_ctx:
            yc_c = _cattn(zc, heads, wbr, (blk_k, blk_v, blk_q), blk_gate + 2)
            ya_c = _sgu(zc, w_sgu[l], b_sgu[l], wbr, blk_uv, blk_gate + 0, _pick_tile(lc, 512))
            yb_c = _fnet(zc, w_fnet[l], b_fnet[l], dft_c, wbr, blk_b, blk_gate + 1, _pick_tile(lc, 512))
            ys_c = [y.reshape(bsz * lc, wbr) for y in (ya_c, yb_c, yc_c, yd_c)]
            m_c = _merge(ct, mod_c, ys_c, w_m[l], b_m, w_up_b[l], tm_c, _pick_tile(d, 512))
            ct = _out_proj(m_c, ct, mod_c, w_o_b[l], ln_g[l], ln_b[l], alpha, tm_c)
        xt = xt_new
    return xt.reshape(bsz, n, d)
```

```python
import functools
import math

import jax
import jax.numpy as jnp
import numpy as np
from jax import lax
from jax.experimental import pallas as pl
from jax.experimental.pallas import tpu as pltpu

GRID_W = 64
N_BRANCH = 4
LN_EPS = 1e-6
NEG_INF = -1e30
Q_ROWS = 4
SUBLANES = 8
LANES = 128
S5_CHUNK = 64
VMEM_LIMIT = 56 * 1024 * 1024

F32 = jnp.float32
BF16 = jnp.bfloat16
HIGHEST = lax.Precision.HIGHEST


def _cparams(sem):
    return pltpu.CompilerParams(dimension_semantics=sem, vmem_limit_bytes=VMEM_LIMIT)


def _dot(a, b):
    return jnp.dot(a, b, preferred_element_type=F32)


def _dot_t(a, b):
    return lax.dot_general(a, b, (((1,), (1,)), ((), ())), preferred_element_type=F32)


def _ln(x):
    mu = jnp.mean(x, axis=-1, keepdims=True)
    xc = x - mu
    var = jnp.mean(xc * xc, axis=-1, keepdims=True)
    return xc * lax.rsqrt(var + LN_EPS)


def _silu(x):
    return x * jax.nn.sigmoid(x)


def _pick_tile(n, target):
    t = min(n, target)
    while n % t:
        t //= 2
    return t


def _mod_spec(d, tiles_per_mod):
    return pl.BlockSpec((1, 1, 3 * d), lambda i, *_: (i // tiles_per_mod, 0, 0))


def _ada_kernel(s_ref, w_ref, b_ref, o_ref):
    s = _silu(s_ref[...]).astype(BF16)
    o_ref[0] = _dot(s, w_ref[0].astype(BF16)) + b_ref[0]


def _ada_all(cond, w_ada, b_ada):
    depth, d, d3 = w_ada.shape
    r = cond.shape[0]
    tn = _pick_tile(d3, 1024)
    return pl.pallas_call(
        _ada_kernel,
        grid=(depth, d3 // tn),
        in_specs=[
            pl.BlockSpec((r, d), lambda l, j: (0, 0)),
            pl.BlockSpec((1, d, tn), lambda l, j: (l, 0, j)),
            pl.BlockSpec((1, 1, tn), lambda l, j: (l, 0, j)),
        ],
        out_specs=pl.BlockSpec((1, r, tn), lambda l, j: (l, 0, j)),
        out_shape=jax.ShapeDtypeStruct((depth, r, d3), F32),
        compiler_params=_cparams(("arbitrary", "arbitrary")),
        name="ada_proj",
    )(cond, w_ada, b_ada.reshape(depth, 1, d3))


def _modulated(x, mod_ref, d):
    return _ln(x) * (1.0 + mod_ref[0, :, d:2 * d]) + mod_ref[0, :, 0:d]


def _modulate_kernel(x_ref, mod_ref, o_ref, *, d):
    o_ref[...] = _modulated(x_ref[...], mod_ref, d).astype(BF16)


def _modulate(x2, mod, tm):
    t, d = x2.shape
    tiles_per_mod = (t // mod.shape[0]) // tm
    return pl.pallas_call(
        functools.partial(_modulate_kernel, d=d),
        grid=(t // tm,),
        in_specs=[pl.BlockSpec((tm, d), lambda i: (i, 0)), _mod_spec(d, tiles_per_mod)],
        out_specs=pl.BlockSpec((tm, d), lambda i: (i, 0)),
        out_shape=jax.ShapeDtypeStruct((t, d), BF16),
        compiler_params=_cparams(("parallel",)),
        name="modulate",
    )(x2, mod)


def _inproj_kernel(h_ref, w_ref, b_ref, o_ref):
    o_ref[...] = (_dot(h_ref[...], w_ref[...]) + b_ref[...]).astype(o_ref.dtype)


def _in_proj(h2, w, b, ncols, tm, tn, dtype):
    t, d = h2.shape
    return pl.pallas_call(
        _inproj_kernel,
        grid=(t // tm, ncols // tn),
        in_specs=[
            pl.BlockSpec((tm, d), lambda i, j: (i, 0)),
            pl.BlockSpec((d, tn), lambda i, j: (0, j)),
            pl.BlockSpec((1, tn), lambda i, j: (0, j)),
        ],
        out_specs=pl.BlockSpec((tm, tn), lambda i, j: (i, j)),
        out_shape=jax.ShapeDtypeStruct((t, ncols), dtype),
        compiler_params=_cparams(("parallel", "arbitrary")),
        name="in_proj",
    )(h2, w, b)


def _sgu_kernel(uv_ref, g_ref, w_ref, b_ref, o_ref, *, wbr, chunk, groups):
    uv = jax.nn.gelu(uv_ref[0])
    u = uv[:, :wbr]
    v = _ln(uv[:, wbr:]).astype(BF16)
    gate = _silu(g_ref[0])
    gd = wbr // groups
    for k in range(uv.shape[0] // chunk):
        rows = slice(k * chunk, (k + 1) * chunk)
        for g in range(groups):
            cols = slice(g * gd, (g + 1) * gd)
            mixed = _dot(w_ref[g], v[rows, cols]) + b_ref[g]
            o_ref[0, rows, cols] = (u[rows, cols] * mixed * gate[rows, cols]).astype(BF16)


def _sgu(z3, w_sgu, b_sgu, wbr, uv_blk, gate_blk, ta):
    bsz, n, _ = z3.shape
    groups, chunk, _ = w_sgu.shape
    return pl.pallas_call(
        functools.partial(_sgu_kernel, wbr=wbr, chunk=chunk, groups=groups),
        grid=(bsz, n // ta),
        in_specs=[
            pl.BlockSpec((1, ta, 2 * wbr), lambda b, i: (b, i, uv_blk)),
            pl.BlockSpec((1, ta, wbr), lambda b, i: (b, i, gate_blk)),
            pl.BlockSpec((groups, chunk, chunk), lambda b, i: (0, 0, 0)),
            pl.BlockSpec((groups, chunk, 1), lambda b, i: (0, 0, 0)),
        ],
        out_specs=pl.BlockSpec((1, ta, wbr), lambda b, i: (b, i, 0)),
        out_shape=jax.ShapeDtypeStruct((bsz, n, wbr), BF16),
        compiler_params=_cparams(("parallel", "arbitrary")),
        name="sgu_branch",
    )(z3, z3, w_sgu.astype(BF16), b_sgu.reshape(groups, chunk, 1))


def _dft_tables(n):
    k = jnp.arange(n, dtype=jnp.int32)
    r = (k[:, None] * k[None, :]) % n
    ang = r.astype(F32) * (2.0 * math.pi / n)
    return jnp.cos(ang).astype(BF16), jnp.sin(ang).astype(BF16)


def _fnet_kernel(z_ref, g_ref, cc_ref, sc_ref, cn_ref, sn_ref, wf_ref, bf_ref, o_ref, p_ref, q_ref,
                 *, groups, norm):
    wbr = z_ref.shape[2]
    gd = wbr // groups

    @pl.when(pl.program_id(1) == 0)
    def _():
        for g in range(groups):
            cols = slice(g * gd, (g + 1) * gd)
            zg = z_ref[0, :, cols]
            p_ref[:, cols] = _dot(zg, cc_ref[...]).astype(BF16)
            q_ref[:, cols] = _dot(zg, sc_ref[...]).astype(BF16)

    f = (_dot(cn_ref[...], p_ref[...]) - _dot(sn_ref[...], q_ref[...])) * norm
    gate = _silu(g_ref[0])
    for g in range(groups):
        cols = slice(g * gd, (g + 1) * gd)
        y = _dot(f[:, cols].astype(BF16), wf_ref[g]) + bf_ref[:, cols]
        o_ref[0, :, cols] = (y * gate[:, cols]).astype(BF16)


def _fnet(zb3, zf3, w_f, b_f, tables, wbr, z_blk, gate_blk, tk):
    bsz, n, _ = zb3.shape
    groups, gd, _ = w_f.shape
    cn, sn, cc, sc = tables
    return pl.pallas_call(
        functools.partial(_fnet_kernel, groups=groups, norm=1.0 / math.sqrt(n * gd)),
        grid=(bsz, n // tk),
        in_specs=[
            pl.BlockSpec((1, n, wbr), lambda b, k: (b, 0, z_blk)),
            pl.BlockSpec((1, tk, wbr), lambda b, k: (b, k, gate_blk)),
            pl.BlockSpec((gd, gd), lambda b, k: (0, 0)),
            pl.BlockSpec((gd, gd), lambda b, k: (0, 0)),
            pl.BlockSpec((tk, n), lambda b, k: (k, 0)),
            pl.BlockSpec((tk, n), lambda b, k: (k, 0)),
            pl.BlockSpec((groups, gd, gd), lambda b, k: (0, 0, 0)),
            pl.BlockSpec((1, wbr), lambda b, k: (0, 0)),
        ],
        out_specs=pl.BlockSpec((1, tk, wbr), lambda b, k: (b, k, 0)),
        out_shape=jax.ShapeDtypeStruct((bsz, n, wbr), BF16),
        scratch_shapes=[pltpu.VMEM((n, wbr), BF16), pltpu.VMEM((n, wbr), BF16)],
        compiler_params=_cparams(("parallel", "arbitrary")),
        name="fnet_branch",
    )(zb3, zf3, cc, sc, cn, sn, w_f.astype(BF16), b_f.reshape(1, wbr))


def _nattn_plan(rows, width, win_h, win_w):
    kh, kw = min(win_h, rows), min(win_w, width)
    qb = Q_ROWS
    wr = min(kh + qb - 1, rows)
    nblk = rows // qb
    r0 = np.arange(nblk) * qb
    ks = np.clip(r0 - kh // 2, 0, rows - wr)
    r = r0[:, None] + np.arange(qb)[None]
    rs = np.clip(r - kh // 2, 0, rows - kh)
    kr = ks[:, None] + np.arange(wr)[None]
    row_ok = (kr[:, None, :] >= rs[:, :, None]) & (kr[:, None, :] < rs[:, :, None] + kh)
    assert row_ok.sum(-1).min() == kh
    dri = np.where(row_ok, kr[:, None, :] - r[:, :, None] + win_h - 1, 0)
    key = np.concatenate([dri.reshape(nblk, -1), row_ok.reshape(nblk, -1)], axis=1)
    _, first, inv = np.unique(key, axis=0, return_index=True, return_inverse=True)
    qc = np.arange(width)
    col_start = np.clip(qc - kw // 2, 0, width - kw)
    in_win = (qc[None, :] >= col_start[:, None]) & (qc[None, :] < col_start[:, None] + kw)
    dci = np.clip(qc[None, :] - qc[:, None], -(kw - 1), kw - 1) + win_w - 1
    return dict(qb=qb, wr=wr, nblk=nblk, ks=ks.astype(np.int32), pat=inv.reshape(-1).astype(np.int32),
                dri=dri[first], row_ok=row_ok[first], in_win=in_win, dci=dci)


def _nattn_bias(rpb, plan, width):
    dri, row_ok = plan["dri"], plan["row_ok"]
    npat, qb, wr = dri.shape
    heads, _, n_dc = rpb.shape
    rows = jnp.take(rpb.astype(F32), jnp.asarray(dri.reshape(-1)), axis=1)
    onehot = (plan["dci"].reshape(-1)[None, :] == np.arange(n_dc)[:, None]).astype(np.float32)
    vals = jnp.einsum("hrc,cw->hrw", rows, jnp.asarray(onehot), precision=HIGHEST)
    vals = vals.reshape(heads, npat, qb, wr, width, width).transpose(1, 0, 2, 4, 3, 5)
    valid = row_ok[:, None, :, None, :, None] & plan["in_win"][None, None, None, :, None, :]
    vals = jnp.where(jnp.asarray(valid), vals, NEG_INF)
    return vals.reshape(npat, heads, qb * width, wr * width)


def _softmax_pv(s_band, s_ctx, v_band, v_ctx):
    m = jnp.maximum(jnp.max(s_band, axis=-1, keepdims=True), jnp.max(s_ctx, axis=-1, keepdims=True))
    p_band = jnp.exp(s_band - m)
    p_ctx = jnp.exp(s_ctx - m)
    den = jnp.sum(p_band, axis=-1, keepdims=True) + jnp.sum(p_ctx, axis=-1, keepdims=True)
    out = _dot(p_band.astype(BF16), v_band) + _dot(p_ctx.astype(BF16), v_ctx)
    return out / den


def _nattn_kernel(ks_ref, pat_ref, q_ref, k_ref, v_ref, ck_ref, cv_ref, g_ref, bias_ref, o_ref,
                  *, heads, width, nkeys):
    blk = pl.program_id(1)
    start = pl.multiple_of(ks_ref[blk] * width, width)
    pat = pat_ref[blk]
    dh = q_ref.shape[2] // heads
    scale = dh ** -0.5
    gate = _silu(g_ref[0])
    for h in range(heads):
        cols = slice(h * dh, (h + 1) * dh)
        kb = k_ref[0, pl.ds(start, nkeys), cols]
        vb = v_ref[0, pl.ds(start, nkeys), cols]
        s_band = _dot_t(q_ref[0, :, cols], kb) * scale + bias_ref[pat, h]
        s_ctx = _dot_t(q_ref[0, :, cols], ck_ref[0, :, cols]) * scale
        out = _softmax_pv(s_band, s_ctx, vb, cv_ref[0, :, cols])
        o_ref[0, :, cols] = (out * gate[:, cols]).astype(BF16)


def _nattn(zxb, zcb, zxf, bias, plan, heads, wbr, blks, gate_blk):
    bsz, n, _ = zxb.shape
    lc = zcb.shape[1]
    k_blk, v_blk, q_blk = blks
    nq = plan["qb"] * GRID_W
    nkeys = plan["wr"] * GRID_W
    grid_spec = pltpu.PrefetchScalarGridSpec(
        num_scalar_prefetch=2,
        grid=(bsz, plan["nblk"]),
        in_specs=[
            pl.BlockSpec((1, nq, wbr), lambda b, i, *_: (b, i, q_blk)),
            pl.BlockSpec((1, n, wbr), lambda b, i, *_: (b, 0, k_blk)),
            pl.BlockSpec((1, n, wbr), lambda b, i, *_: (b, 0, v_blk)),
            pl.BlockSpec((1, lc, wbr), lambda b, i, *_: (b, 0, k_blk)),
            pl.BlockSpec((1, lc, wbr), lambda b, i, *_: (b, 0, v_blk)),
            pl.BlockSpec((1, nq, wbr), lambda b, i, *_: (b, i, gate_blk)),
            pl.BlockSpec(bias.shape, lambda b, i, *_: (0, 0, 0, 0)),
        ],
        out_specs=pl.BlockSpec((1, nq, wbr), lambda b, i, *_: (b, i, 0)),
    )
    return pl.pallas_call(
        functools.partial(_nattn_kernel, heads=heads, width=GRID_W, nkeys=nkeys),
        grid_spec=grid_spec,
        out_shape=jax.ShapeDtypeStruct((bsz, n, wbr), BF16),
        compiler_params=_cparams(("parallel", "arbitrary")),
        name="nattn_branch",
    )(jnp.asarray(plan["ks"]), jnp.asarray(plan["pat"]), zxb, zxb, zxb, zcb, zcb, zxf, bias)


def _cattn_kernel(q_ref, k_ref, v_ref, g_ref, o_ref, *, heads):
    dh = q_ref.shape[2] // heads
    scale = dh ** -0.5
    gate = _silu(g_ref[0])
    for h in range(heads):
        cols = slice(h * dh, (h + 1) * dh)
        s = _dot_t(q_ref[0, :, cols], k_ref[0, :, cols]) * scale
        p = jnp.exp(s - jnp.max(s, axis=-1, keepdims=True))
        out = _dot(p.astype(BF16), v_ref[0, :, cols]) / jnp.sum(p, axis=-1, keepdims=True)
        o_ref[0, :, cols] = (out * gate[:, cols]).astype(BF16)


def _cattn(zcb, zcf, heads, wbr, blks, gate_blk):
    bsz, lc, _ = zcb.shape
    k_blk, v_blk, q_blk = blks
    spec = lambda blk: pl.BlockSpec((1, lc, wbr), lambda b: (b, 0, blk))
    return pl.pallas_call(
        functools.partial(_cattn_kernel, heads=heads),
        grid=(bsz,),
        in_specs=[spec(q_blk), spec(k_blk), spec(v_blk), spec(gate_blk)],
        out_specs=spec(0),
        out_shape=jax.ShapeDtypeStruct((bsz, lc, wbr), BF16),
        compiler_params=_cparams(("parallel",)),
        name="cattn_branch",
    )(zcb, zcb, zcb, zcf)


def _s5_discretise(a_re, a_im, log_dt, b_re, b_im):
    dt = jnp.exp(log_dt)[..., None]
    lr, li = a_re * dt, a_im * dt
    mag = jnp.exp(lr)
    abar_re, abar_im = mag * jnp.cos(li), mag * jnp.sin(li)
    den = a_re * a_re + a_im * a_im
    f_re = ((abar_re - 1.0) * a_re + abar_im * a_im) / den
    f_im = (abar_im * a_re - (abar_re - 1.0) * a_im) / den
    bb_re = f_re[..., None] * b_re - f_im[..., None] * b_im
    bb_im = f_re[..., None] * b_im + f_im[..., None] * b_re
    return lr, li, bb_re, bb_im


def _toeplitz(kfull, t):
    lead = kfull.shape[:-2]
    h = kfull.shape[-1]
    pad = jnp.concatenate([kfull, jnp.zeros(lead + (1, h), kfull.dtype)], axis=-2)
    tiled = jnp.broadcast_to(pad[..., None, :, :], lead + (t, 2 * t, h)).reshape(lead + (2 * t * t, h))
    skew = tiled[..., :t * (2 * t - 1), :].reshape(lead + (t, 2 * t - 1, h))
    return skew[..., t - 1:, :]


def _s5_params(a_re, a_im, log_dt, b_re, b_im, c_re, c_im, t):
    f32 = lambda v: v.astype(F32)
    a_re, a_im, log_dt, b_re, b_im, c_re, c_im = map(f32, (a_re, a_im, log_dt, b_re, b_im, c_re, c_im))
    _, g, p, h = b_re.shape
    lr, li, bb_re, bb_im = _s5_discretise(a_re, a_im, log_dt, b_re, b_im)
    steps = jnp.arange(t + 1, dtype=F32)[:, None, None, None]
    mag = jnp.exp(steps * lr)
    pw_re, pw_im = mag * jnp.cos(steps * li), mag * jnp.sin(steps * li)

    ca_re = c_re[None] * pw_re[:t, :, :, None, :] - c_im[None] * pw_im[:t, :, :, None, :]
    ca_im = c_re[None] * pw_im[:t, :, :, None, :] + c_im[None] * pw_re[:t, :, :, None, :]
    kern = (jnp.einsum("kdghp,dgpi->kdgih", ca_re, bb_re, precision=HIGHEST)
            - jnp.einsum("kdghp,dgpi->kdgih", ca_im, bb_im, precision=HIGHEST))
    kf, kb = kern[:, 0], kern[:, 1]
    kfull = jnp.concatenate([kb[1:][::-1], (kf[0] + kb[0])[None], kf[1:]], axis=0)
    kfull = kfull.transpose(1, 2, 0, 3)
    tz = _toeplitz(kfull, t).reshape(g, h * t, t * h)

    ef_pw = (pw_re[:t, 0][::-1], pw_im[:t, 0][::-1])
    eb_pw = (pw_re[:t, 1], pw_im[:t, 1])

    def drive(pw, d):
        re = pw[0][:, :, :, None] * bb_re[d][None] - pw[1][:, :, :, None] * bb_im[d][None]
        im = pw[0][:, :, :, None] * bb_im[d][None] + pw[1][:, :, :, None] * bb_re[d][None]
        to_rows = lambda v: v.transpose(1, 3, 0, 2).reshape(g, h * t, p)
        return to_rows(re), to_rows(im)

    ef, eb = drive(ef_pw, 0), drive(eb_pw, 1)
    emat = jnp.concatenate([ef[0], eb[0], ef[1], eb[1]], axis=-1)

    def readout(pw, d):
        re = c_re[d][None] * pw[0][:, :, None, :] - c_im[d][None] * pw[1][:, :, None, :]
        im = c_re[d][None] * pw[1][:, :, None, :] + c_im[d][None] * pw[0][:, :, None, :]
        to_cols = lambda v: v.transpose(1, 3, 0, 2).reshape(g, p, t * h)
        return to_cols(re), to_cols(-im)

    rf = readout((pw_re[1:, 0], pw_im[1:, 0]), 0)
    rb = readout((pw_re[1:, 1][::-1], pw_im[1:, 1][::-1]), 1)
    rmat = jnp.concatenate([rf[0], rb[0], rf[1], rb[1]], axis=1)
    at = jnp.concatenate([pw_re[t, 0], pw_re[t, 1], pw_im[t, 0], pw_im[t, 1]], axis=-1)
    return dict(tz=tz.astype(BF16), emat=emat.astype(BF16), rmat=rmat.astype(BF16), at=at.reshape(g, 1, 4 * p))


def _s5_kernel(u_ref, e_ref, at_ref, h0_ref, tz_ref, r_ref, y_ref, ht_ref, loc_ref, sf_ref, sb_ref, *, nc, bsz, p):
    p2 = 2 * p
    u = u_ref[0]
    loc_ref[...] = _dot(u, e_ref[0])
    is_f = lax.broadcasted_iota(jnp.int32, (1, p2), 1) < p
    at_re, at_im = at_ref[0, :, :p2], at_ref[0, :, p2:]
    s_re, s_im = h0_ref[0, :, :p2], h0_ref[0, :, p2:]
    for c in range(nc):
        rf = slice(c * bsz, (c + 1) * bsz)
        rb = slice((nc - 1 - c) * bsz, (nc - c) * bsz)
        sf_ref[rf, :p2] = s_re
        sf_ref[rf, p2:] = s_im
        sb_ref[rb, :p2] = s_re
        sb_ref[rb, p2:] = s_im
        cur_re = jnp.where(is_f, loc_ref[rf, :p2], loc_ref[rb, :p2])
        cur_im = jnp.where(is_f, loc_ref[rf, p2:], loc_ref[rb, p2:])
        s_re, s_im = at_re * s_re - at_im * s_im + cur_re, at_re * s_im + at_im * s_re + cur_im
    ht_ref[0, :, :p2] = s_re
    ht_ref[0, :, p2:] = s_im
    is_f2 = jnp.concatenate([is_f, is_f], axis=1)
    s_in = jnp.where(is_f2, sf_ref[...], sb_ref[...]).astype(BF16)
    y_ref[0] = _dot(u, tz_ref[0]) + _dot(s_in, r_ref[0])


def _s5_core(u3, prm, h0, nc):
    bsz, n, wbr = u3.shape
    g, th, _ = prm["tz"].shape
    t = n // nc
    h = th // t
    p4 = prm["at"].shape[-1]
    assert bsz % SUBLANES == 0 and g * h == wbr
    rows = nc * bsz
    ug = u3.reshape(bsz, nc, t, g, h).transpose(3, 1, 0, 4, 2).reshape(g, rows, th).astype(BF16)
    grp = lambda shape: pl.BlockSpec((1,) + shape, lambda i: (i, 0, 0))
    y, ht = pl.pallas_call(
        functools.partial(_s5_kernel, nc=nc, bsz=bsz, p=p4 // 4),
        grid=(g,),
        in_specs=[grp((rows, th)), grp((th, p4)), grp((1, p4)), grp((bsz, p4)), grp((th, th)), grp((p4, th))],
        out_specs=[grp((rows, th)), grp((bsz, p4))],
        out_shape=[jax.ShapeDtypeStruct((g, rows, th), F32), jax.ShapeDtypeStruct((g, bsz, p4), F32)],
        scratch_shapes=[pltpu.VMEM((rows, p4), F32), pltpu.VMEM((rows, p4), F32), pltpu.VMEM((rows, p4), F32)],
        compiler_params=_cparams(("parallel",)),
        name="s5_core",
    )(ug, prm["emat"], prm["at"], h0, prm["tz"], prm["rmat"])
    y = y.reshape(g, nc, bsz, t, h).transpose(2, 1, 3, 0, 4).reshape(bsz, n, wbr)
    return y, ht


def _s5_out_kernel(y_ref, u_ref, g_ref, d_ref, wg_ref, bg_ref, o_ref):
    tot = jax.nn.gelu(y_ref[...] + d_ref[...] * u_ref[...])
    glu = jax.nn.sigmoid(_dot(tot.astype(BF16), wg_ref[...]) + bg_ref[...])
    o_ref[...] = (tot * glu * _silu(g_ref[...])).astype(BF16)


def _s5_out(y2, z2, d, w_glu, b_glu, wbr, u_blk, gate_blk, tm):
    t = y2.shape[0]
    const = lambda shape: pl.BlockSpec(shape, lambda i: (0, 0))
    return pl.pallas_call(
        _s5_out_kernel,
        grid=(t // tm,),
        in_specs=[
            pl.BlockSpec((tm, wbr), lambda i: (i, 0)),
            pl.BlockSpec((tm, wbr), lambda i: (i, u_blk)),
            pl.BlockSpec((tm, wbr), lambda i: (i, gate_blk)),
            const((1, wbr)), const((wbr, wbr)), const((1, wbr)),
        ],
        out_specs=pl.BlockSpec((tm, wbr), lambda i: (i, 0)),
        out_shape=jax.ShapeDtypeStruct((t, wbr), BF16),
        compiler_params=_cparams(("parallel",)),
        name="s5_out",
    )(y2, z2, z2, d.reshape(1, wbr).astype(F32), w_glu.astype(BF16), b_glu.reshape(1, wbr).astype(F32))


def _merge_kernel(h_ref, y0, y1, y2, y3, w0, w1, w2, w3, bm_ref, u0, u1, u2, u3, o_ref):
    h = h_ref[...]
    acc = None
    for i, (y, w, u) in enumerate(zip((y0, y1, y2, y3), (w0, w1, w2, w3), (u0, u1, u2, u3))):
        term = jax.nn.sigmoid(_dot(h, w[...]) + bm_ref[i]) * _dot(y[...], u[0])
        acc = term if acc is None else acc + term
    o_ref[...] = acc.astype(BF16)


def _merge(h2, ys, w_m, b_m, w_up, tm, tn):
    t, d = h2.shape
    wbr = ys[0].shape[1]
    ns = d // tn
    y_spec = pl.BlockSpec((tm, wbr), lambda i, s: (i, 0))
    w_specs = [pl.BlockSpec((d, tn), functools.partial(lambda i, s, br: (0, br * ns + s), br=br))
               for br in range(N_BRANCH)]
    u_specs = [pl.BlockSpec((1, wbr, tn), functools.partial(lambda i, s, br: (br, 0, s), br=br))
               for br in range(N_BRANCH)]
    return pl.pallas_call(
        _merge_kernel,
        grid=(t // tm, ns),
        in_specs=[
            pl.BlockSpec((tm, d), lambda i, s: (i, 0)),
            y_spec, y_spec, y_spec, y_spec,
            *w_specs,
            pl.BlockSpec((N_BRANCH, 1, tn), lambda i, s: (0, 0, s)),
            *u_specs,
        ],
        out_specs=pl.BlockSpec((tm, tn), lambda i, s: (i, s)),
        out_shape=jax.ShapeDtypeStruct((t, d), BF16),
        compiler_params=_cparams(("parallel", "arbitrary")),
        name="merge",
    )(h2, *ys, w_m, w_m, w_m, w_m, b_m, w_up, w_up, w_up, w_up)


def _outproj_kernel(*refs, d, alpha, sub, emit_h):
    if emit_h:
        m_ref, x_ref, mod_ref, modn_ref, w_ref, g_ref, b_ref, o_ref, h_ref = refs
    else:
        m_ref, x_ref, mod_ref, w_ref, g_ref, b_ref, o_ref = refs
    gate = mod_ref[0, :, 2 * d:3 * d]
    for r in range(m_ref.shape[0] // sub):
        rows = slice(r * sub, (r + 1) * sub)
        out = _dot(m_ref[rows, :], w_ref[...])
        xn = _ln(alpha * x_ref[rows, :] + gate * out) * g_ref[...] + b_ref[...]
        o_ref[rows, :] = xn
        if emit_h:
            h_ref[rows, :] = _modulated(xn, modn_ref, d).astype(BF16)


def _out_proj(m2, x2, mod, mod_next, w_o, ln_g, ln_b, alpha, tm):
    t, d = x2.shape
    tiles_per_mod = (t // mod.shape[0]) // tm
    emit_h = mod_next is not None
    row = pl.BlockSpec((tm, d), lambda i: (i, 0))
    vec = pl.BlockSpec((1, d), lambda i: (0, 0))
    mods = [_mod_spec(d, tiles_per_mod)] * (2 if emit_h else 1)
    mod_args = (mod, mod_next) if emit_h else (mod,)
    out = pl.pallas_call(
        functools.partial(_outproj_kernel, d=d, alpha=alpha, sub=_pick_tile(tm, 256), emit_h=emit_h),
        grid=(t // tm,),
        in_specs=[row, row, *mods, pl.BlockSpec((d, d), lambda i: (0, 0)), vec, vec],
        out_specs=[row, row] if emit_h else row,
        out_shape=([jax.ShapeDtypeStruct((t, d), F32), jax.ShapeDtypeStruct((t, d), BF16)] if emit_h
                   else jax.ShapeDtypeStruct((t, d), F32)),
        compiler_params=_cparams(("parallel",)),
        name="out_proj",
    )(m2, x2, *mod_args, w_o, ln_g.reshape(1, d), ln_b.reshape(1, d))
    return out if emit_h else (out, None)


def kernel(x, c, ctx, c_ctx, w_ada, b_ada, w_in, b_in, w_sgu, b_sgu, w_fnet, b_fnet, rpb, s5_a_re, s5_a_im,
           s5_log_dt, s5_b_re, s5_b_im, s5_c_re, s5_c_im, s5_d, s5_w_glu, s5_b_glu, w_up, w_o, ln_g, ln_b):
    bsz, n, d = x.shape
    lc = ctx.shape[1]
    depth = w_ada.shape[0]
    wbr = d // N_BRANCH
    heads = rpb.shape[1]
    win_h, win_w = (rpb.shape[2] + 1) // 2, (rpb.shape[3] + 1) // 2
    rows = n // GRID_W
    alpha = (2 * depth) ** 0.25
    s5_g, s5_p = s5_a_re.shape[2], s5_a_re.shape[3]

    W = wbr
    take = lambda a, spans: jnp.concatenate([a[..., lo * W:hi * W] for lo, hi in spans], axis=-1)
    spans16 = ((1, 4), (6, 7))
    spans32 = ((4, 6), (0, 1), (7, 11))
    off_merge = 11 * W
    blk_k, blk_v, blk_q, blk_b = 0, 1, 2, 3
    blk_uv, blk_u, blk_gate = 0, 2, 3
    n16, n32 = 4 * W, 7 * W

    n_rows = -(-(bsz + 1) // 8) * 8
    cond = jnp.zeros((n_rows, d), F32).at[:bsz].set(c).at[bsz].set(c_ctx)
    ada = _ada_all(cond, w_ada, b_ada)
    mod_x = [ada[l, :bsz].reshape(bsz, 1, 3 * d) for l in range(depth)]
    mod_c = [ada[l, bsz:bsz + 1].reshape(1, 1, 3 * d) for l in range(depth)]

    w16 = take(w_in, spans16).astype(BF16)
    w32 = take(w_in, spans32).astype(BF16)
    w_m = w_in[:, :, off_merge:].astype(BF16)
    w_up_b = w_up.astype(BF16)
    w_o_b = w_o.astype(BF16)

    plan = _nattn_plan(rows, GRID_W, win_h, win_w)
    gd_f = w_fnet.shape[2]
    cc, sc = _dft_tables(gd_f)
    dft_x = _dft_tables(n) + (cc, sc)
    dft_c = _dft_tables(lc) + (cc, sc)

    t_s5 = _pick_tile(math.gcd(n, lc), S5_CHUNK)
    tx, tc_all = bsz * n, bsz * lc
    tm_x, tm_c = _pick_tile(n, 1024), _pick_tile(tc_all, 1024)
    tm_o_x, tm_o_c = _pick_tile(n, 512), _pick_tile(tc_all, 512)
    tn16 = _pick_tile(n16, 1024)
    tn32 = n32 // 2 if (n32 // 2) % LANES == 0 else W

    xt = x.reshape(tx, d)
    ct = ctx.reshape(tc_all, d)
    hx = _modulate(xt, mod_x[0], tm_o_x)
    hc = _modulate(ct, mod_c[0], tm_o_c)
    for l in range(depth):
        need_ctx = l < depth - 1
        b16 = take(b_in[l], spans16).reshape(1, n16)
        b32 = take(b_in[l], spans32).reshape(1, n32)
        b_m = b_in[l, off_merge:].reshape(N_BRANCH, 1, d)

        zxb = _in_proj(hx, w16[l], b16, n16, tm_x, tn16, BF16).reshape(bsz, n, n16)
        zxf = _in_proj(hx, w32[l], b32, n32, tm_x, tn32, F32).reshape(bsz, n, n32)
        c16, c32 = (n16, n32) if need_ctx else (2 * W, 3 * W)
        zcb = _in_proj(hc, w16[l], b16, c16, tm_c, _pick_tile(c16, 1024), BF16).reshape(bsz, lc, c16)
        zcf = _in_proj(hc, w32[l], b32, c32, tm_c, tn32 if need_ctx else W, F32).reshape(bsz, lc, c32)

        prm = _s5_params(s5_a_re[l], s5_a_im[l], s5_log_dt[l], s5_b_re[l], s5_b_im[l], s5_c_re[l], s5_c_im[l], t_s5)
        u_c = zcf[:, :, blk_u * W:(blk_u + 1) * W]
        u_x = zxf[:, :, blk_u * W:(blk_u + 1) * W]
        ys_c, h_ctx = _s5_core(u_c, prm, jnp.zeros((s5_g, bsz, 4 * s5_p), F32), lc // t_s5)
        ys_x, _ = _s5_core(u_x, prm, h_ctx, n // t_s5)
        yd_x = _s5_out(ys_x.reshape(tx, W), zxf.reshape(tx, n32), s5_d[l], s5_w_glu[l], s5_b_glu[l],
                       W, blk_u, blk_gate + 3, tm_x)

        bias = _nattn_bias(rpb[l], plan, GRID_W)
        yc_x = _nattn(zxb, zcb, zxf, bias, plan, heads, W, (blk_k, blk_v, blk_q), blk_gate + 2)
        ya_x = _sgu(zxf, w_sgu[l], b_sgu[l], W, blk_uv, blk_gate + 0, _pick_tile(n, 512))
        yb_x = _fnet(zxb, zxf, w_fnet[l], b_fnet[l], dft_x, W, blk_b, blk_gate + 1, _pick_tile(n, 512))

        ys = [ya_x.reshape(tx, W), yb_x.reshape(tx, W), yc_x.reshape(tx, W), yd_x]
        m = _merge(hx, ys, w_m[l], b_m, w_up_b[l], tm_x, _pick_tile(d, 512))
        xt_new, hx_new = _out_proj(m, xt, mod_x[l], mod_x[l + 1] if need_ctx else None,
                                   w_o_b[l], ln_g[l], ln_b[l], alpha, tm_o_x)

        if need_ctx:
            yd_c = _s5_out(ys_c.reshape(tc_all, W), zcf.reshape(tc_all, n32), s5_d[l], s5_w_glu[l], s5_b_glu[l],
                           W, blk_u, blk_gate + 3, tm_c)
            yc_c = _cattn(zcb, zcf, heads, W, (blk_k, blk_v, blk_q), blk_gate + 2)
            ya_c = _sgu(zcf, w_sgu[l], b_sgu[l], W, blk_uv, blk_gate + 0, _pick_tile(lc, 512))
            yb_c = _fnet(zcb, zcf, w_fnet[l], b_fnet[l], dft_c, W, blk_b, blk_gate + 1, _pick_tile(lc, 512))
            ys_cl = [ya_c.reshape(tc_all, W), yb_c.reshape(tc_all, W), yc_c.reshape(tc_all, W), yd_c]
            m_c = _merge(hc, ys_cl, w_m[l], b_m, w_up_b[l], tm_c, _pick_tile(d, 512))
            ct, hc = _out_proj(m_c, ct, mod_c[l], mod_c[l + 1], w_o_b[l], ln_g[l], ln_b[l], alpha, tm_o_c)
        xt, hx = xt_new, hx_new
    return xt.reshape(bsz, n, d)
```

```python
import functools
import math

import jax
import jax.numpy as jnp
import numpy as np
from jax import lax
from jax.experimental import pallas as pl
from jax.experimental.pallas import tpu as pltpu

GRID_W = 64
N_BRANCH = 4
LN_EPS = 1e-6
NEG_INF = -1e30
Q_ROWS = 4
SUBLANES = 8
LANES = 128
S5_ROWS = 64
VMEM_LIMIT = 56 * 1024 * 1024

F32 = jnp.float32
BF16 = jnp.bfloat16
HIGHEST = lax.Precision.HIGHEST


def _cparams(sem):
    return pltpu.CompilerParams(dimension_semantics=sem, vmem_limit_bytes=VMEM_LIMIT)


def _dot(a, b):
    return jnp.dot(a, b, preferred_element_type=F32)


def _dot_t(a, b):
    return lax.dot_general(a, b, (((1,), (1,)), ((), ())), preferred_element_type=F32)


def _ln(x):
    mu = jnp.mean(x, axis=-1, keepdims=True)
    xc = x - mu
    var = jnp.mean(xc * xc, axis=-1, keepdims=True)
    return xc * lax.rsqrt(var + LN_EPS)


def _silu(x):
    return x * jax.nn.sigmoid(x)


def _pick_tile(n, target):
    t = min(n, target)
    while n % t:
        t //= 2
    return t


def _mod_spec(d, tiles_per_mod):
    return pl.BlockSpec((1, 1, 3 * d), lambda i, *_: (i // tiles_per_mod, 0, 0))


def _ada_kernel(s_ref, w_ref, b_ref, o_ref):
    s = _silu(s_ref[...]).astype(BF16)
    o_ref[0] = _dot(s, w_ref[0].astype(BF16)) + b_ref[0]


def _ada_all(cond, w_ada, b_ada):
    depth, d, d3 = w_ada.shape
    r = cond.shape[0]
    tn = _pick_tile(d3, 1024)
    return pl.pallas_call(
        _ada_kernel,
        grid=(depth, d3 // tn),
        in_specs=[
            pl.BlockSpec((r, d), lambda l, j: (0, 0)),
            pl.BlockSpec((1, d, tn), lambda l, j: (l, 0, j)),
            pl.BlockSpec((1, 1, tn), lambda l, j: (l, 0, j)),
        ],
        out_specs=pl.BlockSpec((1, r, tn), lambda l, j: (l, 0, j)),
        out_shape=jax.ShapeDtypeStruct((depth, r, d3), F32),
        compiler_params=_cparams(("arbitrary", "arbitrary")),
        name="ada_proj",
    )(cond, w_ada, b_ada.reshape(depth, 1, d3))


def _modulated(x, mod_ref, d):
    return _ln(x) * (1.0 + mod_ref[0, :, d:2 * d]) + mod_ref[0, :, 0:d]


def _modulate_kernel(x_ref, mod_ref, o_ref, *, d):
    o_ref[...] = _modulated(x_ref[...], mod_ref, d).astype(BF16)


def _modulate(x2, mod, tm):
    t, d = x2.shape
    tiles_per_mod = (t // mod.shape[0]) // tm
    return pl.pallas_call(
        functools.partial(_modulate_kernel, d=d),
        grid=(t // tm,),
        in_specs=[pl.BlockSpec((tm, d), lambda i: (i, 0)), _mod_spec(d, tiles_per_mod)],
        out_specs=pl.BlockSpec((tm, d), lambda i: (i, 0)),
        out_shape=jax.ShapeDtypeStruct((t, d), BF16),
        compiler_params=_cparams(("parallel",)),
        name="modulate",
    )(x2, mod)


def _inproj_kernel(h_ref, w_ref, b_ref, o_ref):
    o_ref[...] = (_dot(h_ref[...], w_ref[...]) + b_ref[...]).astype(o_ref.dtype)


def _in_proj(h2, w, b, ncols, tm, tn, dtype):
    t, d = h2.shape
    return pl.pallas_call(
        _inproj_kernel,
        grid=(t // tm, ncols // tn),
        in_specs=[
            pl.BlockSpec((tm, d), lambda i, j: (i, 0)),
            pl.BlockSpec((d, tn), lambda i, j: (0, j)),
            pl.BlockSpec((1, tn), lambda i, j: (0, j)),
        ],
        out_specs=pl.BlockSpec((tm, tn), lambda i, j: (i, j)),
        out_shape=jax.ShapeDtypeStruct((t, ncols), dtype),
        compiler_params=_cparams(("parallel", "arbitrary")),
        name="in_proj",
    )(h2, w, b)


def _sgu_kernel(uv_ref, g_ref, w_ref, b_ref, o_ref, *, wbr, chunk, groups):
    uv = jax.nn.gelu(uv_ref[0])
    u = uv[:, :wbr]
    v = _ln(uv[:, wbr:]).astype(BF16)
    gate = _silu(g_ref[0])
    gd = wbr // groups
    for k in range(uv.shape[0] // chunk):
        rows = slice(k * chunk, (k + 1) * chunk)
        for g in range(groups):
            cols = slice(g * gd, (g + 1) * gd)
            mixed = _dot(w_ref[g], v[rows, cols]) + b_ref[g]
            o_ref[0, rows, cols] = (u[rows, cols] * mixed * gate[rows, cols]).astype(BF16)


def _sgu(z3, w_sgu, b_sgu, wbr, uv_blk, gate_blk, ta):
    bsz, n, _ = z3.shape
    groups, chunk, _ = w_sgu.shape
    return pl.pallas_call(
        functools.partial(_sgu_kernel, wbr=wbr, chunk=chunk, groups=groups),
        grid=(bsz, n // ta),
        in_specs=[
            pl.BlockSpec((1, ta, 2 * wbr), lambda b, i: (b, i, uv_blk)),
            pl.BlockSpec((1, ta, wbr), lambda b, i: (b, i, gate_blk)),
            pl.BlockSpec((groups, chunk, chunk), lambda b, i: (0, 0, 0)),
            pl.BlockSpec((groups, chunk, 1), lambda b, i: (0, 0, 0)),
        ],
        out_specs=pl.BlockSpec((1, ta, wbr), lambda b, i: (b, i, 0)),
        out_shape=jax.ShapeDtypeStruct((bsz, n, wbr), BF16),
        compiler_params=_cparams(("parallel", "arbitrary")),
        name="sgu_branch",
    )(z3, z3, w_sgu.astype(BF16), b_sgu.reshape(groups, chunk, 1))


def _dft_tables(n):
    k = jnp.arange(n, dtype=jnp.int32)
    r = (k[:, None] * k[None, :]) % n
    ang = r.astype(F32) * (2.0 * math.pi / n)
    return jnp.cos(ang).astype(BF16), jnp.sin(ang).astype(BF16)


def _fnet_kernel(z_ref, g_ref, cc_ref, sc_ref, cn_ref, sn_ref, wf_ref, bf_ref, o_ref, p_ref, q_ref,
                 *, groups, norm):
    wbr = z_ref.shape[2]
    gd = wbr // groups

    @pl.when(pl.program_id(1) == 0)
    def _():
        for g in range(groups):
            cols = slice(g * gd, (g + 1) * gd)
            zg = z_ref[0, :, cols]
            p_ref[:, cols] = _dot(zg, cc_ref[...]).astype(BF16)
            q_ref[:, cols] = _dot(zg, sc_ref[...]).astype(BF16)

    f = (_dot(cn_ref[...], p_ref[...]) - _dot(sn_ref[...], q_ref[...])) * norm
    gate = _silu(g_ref[0])
    for g in range(groups):
        cols = slice(g * gd, (g + 1) * gd)
        y = _dot(f[:, cols].astype(BF16), wf_ref[g]) + bf_ref[:, cols]
        o_ref[0, :, cols] = (y * gate[:, cols]).astype(BF16)


def _fnet(zb3, zf3, w_f, b_f, tables, wbr, z_blk, gate_blk, tk):
    bsz, n, _ = zb3.shape
    groups, gd, _ = w_f.shape
    cn, sn, cc, sc = tables
    return pl.pallas_call(
        functools.partial(_fnet_kernel, groups=groups, norm=1.0 / math.sqrt(n * gd)),
        grid=(bsz, n // tk),
        in_specs=[
            pl.BlockSpec((1, n, wbr), lambda b, k: (b, 0, z_blk)),
            pl.BlockSpec((1, tk, wbr), lambda b, k: (b, k, gate_blk)),
            pl.BlockSpec((gd, gd), lambda b, k: (0, 0)),
            pl.BlockSpec((gd, gd), lambda b, k: (0, 0)),
            pl.BlockSpec((tk, n), lambda b, k: (k, 0)),
            pl.BlockSpec((tk, n), lambda b, k: (k, 0)),
            pl.BlockSpec((groups, gd, gd), lambda b, k: (0, 0, 0)),
            pl.BlockSpec((1, wbr), lambda b, k: (0, 0)),
        ],
        out_specs=pl.BlockSpec((1, tk, wbr), lambda b, k: (b, k, 0)),
        out_shape=jax.ShapeDtypeStruct((bsz, n, wbr), BF16),
        scratch_shapes=[pltpu.VMEM((n, wbr), BF16), pltpu.VMEM((n, wbr), BF16)],
        compiler_params=_cparams(("parallel", "arbitrary")),
        name="fnet_branch",
    )(zb3, zf3, cc, sc, cn, sn, w_f.astype(BF16), b_f.reshape(1, wbr))


def _nattn_plan(rows, width, win_h, win_w):
    kh, kw = min(win_h, rows), min(win_w, width)
    qb = Q_ROWS
    wr = min(kh + qb - 1, rows)
    nblk = rows // qb
    r0 = np.arange(nblk) * qb
    ks = np.clip(r0 - kh // 2, 0, rows - wr)
    r = r0[:, None] + np.arange(qb)[None]
    rs = np.clip(r - kh // 2, 0, rows - kh)
    kr = ks[:, None] + np.arange(wr)[None]
    row_ok = (kr[:, None, :] >= rs[:, :, None]) & (kr[:, None, :] < rs[:, :, None] + kh)
    assert row_ok.sum(-1).min() == kh
    dri = np.where(row_ok, kr[:, None, :] - r[:, :, None] + win_h - 1, 0)
    key = np.concatenate([dri.reshape(nblk, -1), row_ok.reshape(nblk, -1)], axis=1)
    _, first, inv = np.unique(key, axis=0, return_index=True, return_inverse=True)
    qc = np.arange(width)
    col_start = np.clip(qc - kw // 2, 0, width - kw)
    in_win = (qc[None, :] >= col_start[:, None]) & (qc[None, :] < col_start[:, None] + kw)
    dci = np.clip(qc[None, :] - qc[:, None], -(kw - 1), kw - 1) + win_w - 1
    return dict(qb=qb, wr=wr, nblk=nblk, ks=ks.astype(np.int32), pat=inv.reshape(-1).astype(np.int32),
                dri=dri[first], row_ok=row_ok[first], in_win=in_win, dci=dci)


def _nattn_bias(rpb, plan, width):
    dri, row_ok = plan["dri"], plan["row_ok"]
    npat, qb, wr = dri.shape
    heads, _, n_dc = rpb.shape
    rows = jnp.take(rpb.astype(F32), jnp.asarray(dri.reshape(-1)), axis=1)
    onehot = (plan["dci"].reshape(-1)[None, :] == np.arange(n_dc)[:, None]).astype(np.float32)
    vals = jnp.einsum("hrc,cw->hrw", rows, jnp.asarray(onehot), precision=HIGHEST)
    vals = vals.reshape(heads, npat, qb, wr, width, width).transpose(1, 0, 2, 4, 3, 5)
    valid = row_ok[:, None, :, None, :, None] & plan["in_win"][None, None, None, :, None, :]
    vals = jnp.where(jnp.asarray(valid), vals, NEG_INF)
    return vals.reshape(npat, heads, qb * width, wr * width)


def _softmax_pv(s_band, s_ctx, v_band, v_ctx):
    m = jnp.maximum(jnp.max(s_band, axis=-1, keepdims=True), jnp.max(s_ctx, axis=-1, keepdims=True))
    p_band = jnp.exp(s_band - m)
    p_ctx = jnp.exp(s_ctx - m)
    den = jnp.sum(p_band, axis=-1, keepdims=True) + jnp.sum(p_ctx, axis=-1, keepdims=True)
    out = _dot(p_band.astype(BF16), v_band) + _dot(p_ctx.astype(BF16), v_ctx)
    return out / den


def _nattn_kernel(ks_ref, pat_ref, q_ref, k_ref, v_ref, ck_ref, cv_ref, g_ref, bias_ref, o_ref,
                  *, heads, width, nkeys):
    blk = pl.program_id(1)
    start = pl.multiple_of(ks_ref[blk] * width, width)
    pat = pat_ref[blk]
    dh = q_ref.shape[2] // heads
    scale = dh ** -0.5
    gate = _silu(g_ref[0])
    for h in range(heads):
        cols = slice(h * dh, (h + 1) * dh)
        kb = k_ref[0, pl.ds(start, nkeys), cols]
        vb = v_ref[0, pl.ds(start, nkeys), cols]
        s_band = _dot_t(q_ref[0, :, cols], kb) * scale + bias_ref[pat, h]
        s_ctx = _dot_t(q_ref[0, :, cols], ck_ref[0, :, cols]) * scale
        out = _softmax_pv(s_band, s_ctx, vb, cv_ref[0, :, cols])
        o_ref[0, :, cols] = (out * gate[:, cols]).astype(BF16)


def _nattn(zxb, zcb, zxf, bias, plan, heads, wbr, blks, gate_blk):
    bsz, n, _ = zxb.shape
    lc = zcb.shape[1]
    k_blk, v_blk, q_blk = blks
    nq = plan["qb"] * GRID_W
    nkeys = plan["wr"] * GRID_W
    grid_spec = pltpu.PrefetchScalarGridSpec(
        num_scalar_prefetch=2,
        grid=(bsz, plan["nblk"]),
        in_specs=[
            pl.BlockSpec((1, nq, wbr), lambda b, i, *_: (b, i, q_blk)),
            pl.BlockSpec((1, n, wbr), lambda b, i, *_: (b, 0, k_blk)),
            pl.BlockSpec((1, n, wbr), lambda b, i, *_: (b, 0, v_blk)),
            pl.BlockSpec((1, lc, wbr), lambda b, i, *_: (b, 0, k_blk)),
            pl.BlockSpec((1, lc, wbr), lambda b, i, *_: (b, 0, v_blk)),
            pl.BlockSpec((1, nq, wbr), lambda b, i, *_: (b, i, gate_blk)),
            pl.BlockSpec(bias.shape, lambda b, i, *_: (0, 0, 0, 0)),
        ],
        out_specs=pl.BlockSpec((1, nq, wbr), lambda b, i, *_: (b, i, 0)),
    )
    return pl.pallas_call(
        functools.partial(_nattn_kernel, heads=heads, width=GRID_W, nkeys=nkeys),
        grid_spec=grid_spec,
        out_shape=jax.ShapeDtypeStruct((bsz, n, wbr), BF16),
        compiler_params=_cparams(("parallel", "arbitrary")),
        name="nattn_branch",
    )(jnp.asarray(plan["ks"]), jnp.asarray(plan["pat"]), zxb, zxb, zxb, zcb, zcb, zxf, bias)


def _cattn_kernel(q_ref, k_ref, v_ref, g_ref, o_ref, *, heads):
    dh = q_ref.shape[2] // heads
    scale = dh ** -0.5
    gate = _silu(g_ref[0])
    for h in range(heads):
        cols = slice(h * dh, (h + 1) * dh)
        s = _dot_t(q_ref[0, :, cols], k_ref[0, :, cols]) * scale
        p = jnp.exp(s - jnp.max(s, axis=-1, keepdims=True))
        out = _dot(p.astype(BF16), v_ref[0, :, cols]) / jnp.sum(p, axis=-1, keepdims=True)
        o_ref[0, :, cols] = (out * gate[:, cols]).astype(BF16)


def _cattn(zcb, zcf, heads, wbr, blks, gate_blk):
    bsz, lc, _ = zcb.shape
    k_blk, v_blk, q_blk = blks
    spec = lambda blk: pl.BlockSpec((1, lc, wbr), lambda b: (b, 0, blk))
    return pl.pallas_call(
        functools.partial(_cattn_kernel, heads=heads),
        grid=(bsz,),
        in_specs=[spec(q_blk), spec(k_blk), spec(v_blk), spec(gate_blk)],
        out_specs=spec(0),
        out_shape=jax.ShapeDtypeStruct((bsz, lc, wbr), BF16),
        compiler_params=_cparams(("parallel",)),
        name="cattn_branch",
    )(zcb, zcb, zcb, zcf)


def _s5_discretise(a_re, a_im, log_dt, b_re, b_im):
    dt = jnp.exp(log_dt)[..., None]
    lr, li = a_re * dt, a_im * dt
    mag = jnp.exp(lr)
    abar_re, abar_im = mag * jnp.cos(li), mag * jnp.sin(li)
    den = a_re * a_re + a_im * a_im
    f_re = ((abar_re - 1.0) * a_re + abar_im * a_im) / den
    f_im = (abar_im * a_re - (abar_re - 1.0) * a_im) / den
    bb_re = f_re[..., None] * b_re - f_im[..., None] * b_im
    bb_im = f_re[..., None] * b_im + f_im[..., None] * b_re
    return abar_re, abar_im, bb_re, bb_im


def _block_diag(m, gpb):
    two, g, a, b = m.shape
    nb = g // gpb
    m = m.reshape(two, nb, gpb, a, b)
    eye = jnp.eye(gpb, dtype=m.dtype)
    out = m[:, :, :, :, None, :] * eye[None, None, :, None, :, None]
    return out.reshape(two, nb, gpb * a, gpb * b)


def _s5_params(a_re, a_im, log_dt, b_re, b_im, c_re, c_im):
    f32 = lambda t: t.astype(F32)
    a_re, a_im, log_dt, b_re, b_im, c_re, c_im = map(f32, (a_re, a_im, log_dt, b_re, b_im, c_re, c_im))
    _, g, p, h = b_re.shape
    gpb = min(g, max(1, LANES // h))
    abar_re, abar_im, bb_re, bb_im = _s5_discretise(a_re, a_im, log_dt, b_re, b_im)
    tr = lambda t: jnp.swapaxes(t, -1, -2)
    bmat = jnp.concatenate([_block_diag(tr(bb_re), gpb), _block_diag(tr(bb_im), gpb)], axis=-1)
    cmat = jnp.stack([_block_diag(tr(c_re), gpb), _block_diag(-tr(c_im), gpb)], axis=2)
    abar = jnp.concatenate([abar_re.reshape(2, 1, g * p), abar_im.reshape(2, 1, g * p)], axis=-1)
    return dict(bmat=bmat.astype(BF16), cmat=cmat.astype(BF16), abar=abar)


def _interleave_perm(nseq, rows):
    src = np.arange(nseq * rows)
    dst = (src % rows) * nseq + src // rows
    pm = np.zeros((nseq * rows, nseq * rows), np.float32)
    pm[dst, src] = 1.0
    return pm


def _cmul_add(ar, ai, xr, xi, br, bi):
    return ar * xr - ai * xi + br, ar * xi + ai * xr + bi


def _permute_rows_exact(pm, y):
    hi = y.astype(BF16)
    r1 = y - hi.astype(F32)
    mid = r1.astype(BF16)
    lo = (r1 - mid.astype(F32)).astype(BF16)
    return _dot(pm, hi) + _dot(pm, mid) + _dot(pm, lo)


def _s5_kernel(*refs, reverse, final, tiles_per_blk):
    if final:
        (u_ref, h0_ref, pm_ref, bm_ref, cm_ref, a_ref, pmt_ref, yf_ref, g_ref, d_ref, wg_ref, bg_ref,
         o_ref, ht_ref, xs_ref, xb_ref, st_ref) = refs
    else:
        (u_ref, h0_ref, pm_ref, bm_ref, cm_ref, a_ref,
         o_ref, ht_ref, xs_ref, xb_ref, st_ref) = refs
    nseq, tcb, wbr = u_ref.shape
    n_tiles, lt = xs_ref.shape[0] // 2, xs_ref.shape[2]
    sp = n_tiles * lt
    nb = bm_ref.shape[1]
    bw_in, bw_st = wbr // nb, sp // nb
    tpb = bw_st // lt
    re_cols = lambda t: slice(t * lt, (t + 1) * lt)
    im_cols = lambda t: slice(sp + t * lt, sp + (t + 1) * lt)

    @pl.when(pl.program_id(1) == 0)
    def _():
        st_ref[...] = h0_ref[...]

    u = u_ref[...].reshape(nseq * tcb, wbr)
    ub = _dot(pm_ref[...], u.astype(BF16)).astype(BF16)
    for n in range(nb):
        bu = _dot(ub[:, n * bw_in:(n + 1) * bw_in], bm_ref[0, n])
        for t in range(tpb):
            xs_ref[n * tpb + t] = bu[:, t * lt:(t + 1) * lt]
            xs_ref[n_tiles + n * tpb + t] = bu[:, bw_st + t * lt:bw_st + (t + 1) * lt]

    npairs = tcb // 2
    for q in range(n_tiles // tiles_per_blk):
        tiles = range(q * tiles_per_blk, (q + 1) * tiles_per_blk)
        ar = [jnp.broadcast_to(a_ref[0, :, re_cols(t)], (nseq, lt)) for t in tiles]
        ai = [jnp.broadcast_to(a_ref[0, :, im_cols(t)], (nseq, lt)) for t in tiles]

        def step(i, st):
            jj = npairs - 1 - i if reverse else i
            r0 = pl.multiple_of(jj * 2 * nseq, 2 * nseq)
            out = []
            for idx, t in enumerate(tiles):
                dr, di = xs_ref[t, pl.ds(r0, 2 * nseq), :], xs_ref[n_tiles + t, pl.ds(r0, 2 * nseq), :]
                first, second = (slice(nseq, None), slice(0, nseq)) if reverse else (slice(0, nseq), slice(nseq, None))
                x0 = _cmul_add(ar[idx], ai[idx], st[2 * idx], st[2 * idx + 1], dr[first], di[first])
                x1 = _cmul_add(ar[idx], ai[idx], x0[0], x0[1], dr[second], di[second])
                lo, hi = (x1, x0) if reverse else (x0, x1)
                xb_ref[pl.ds(r0, 2 * nseq), re_cols(t)] = jnp.concatenate([lo[0], hi[0]], axis=0).astype(BF16)
                xb_ref[pl.ds(r0, 2 * nseq), im_cols(t)] = jnp.concatenate([lo[1], hi[1]], axis=0).astype(BF16)
                out += [x1[0], x1[1]]
            return tuple(out)

        init = []
        for t in tiles:
            init += [st_ref[:, re_cols(t)], st_ref[:, im_cols(t)]]
        last = lax.fori_loop(0, npairs, step, tuple(init), unroll=2)
        for idx, t in enumerate(tiles):
            st_ref[:, re_cols(t)] = last[2 * idx]
            st_ref[:, im_cols(t)] = last[2 * idx + 1]

    ht_ref[...] = st_ref[...]

    parts = []
    for n in range(nb):
        parts.append(_dot(xb_ref[:, n * bw_st:(n + 1) * bw_st], cm_ref[0, n, 0])
                     + _dot(xb_ref[:, sp + n * bw_st:sp + (n + 1) * bw_st], cm_ref[0, n, 1]))
    y = parts[0] if nb == 1 else jnp.concatenate(parts, axis=-1)

    if final:
        y = _permute_rows_exact(pmt_ref[...], yf_ref[...].reshape(nseq * tcb, wbr) + y)
        tot = jax.nn.gelu(y + d_ref[...] * u)
        glu = jax.nn.sigmoid(_dot(tot.astype(BF16), wg_ref[...]) + bg_ref[...])
        gate = _silu(g_ref[...].reshape(nseq * tcb, wbr))
        o_ref[...] = (tot * glu * gate).astype(BF16).reshape(nseq, tcb, wbr)
    else:
        o_ref[...] = y.reshape(nseq, tcb, wbr)


def _s5_pass(z3, h0, prm, dirn, wbr, u_blk, tcb, final_args=None):
    bsz, n, _ = z3.shape
    assert bsz % SUBLANES == 0 and n % tcb == 0 and tcb % 2 == 0
    nchunks = n // tcb
    reverse = dirn == 1
    final = final_args is not None
    sp2 = prm["abar"].shape[-1]
    bmat, cmat = prm["bmat"], prm["cmat"]
    sp = sp2 // 2
    lt = min(LANES, sp // bmat.shape[1])
    tiles_per_blk = min(sp // lt, 4)
    rows = SUBLANES * tcb
    pm = _interleave_perm(SUBLANES, tcb)
    chunk = (lambda k: nchunks - 1 - k) if reverse else (lambda k: k)
    seq_spec = lambda blk: pl.BlockSpec((SUBLANES, tcb, wbr), lambda g, k: (g, chunk(k), blk))
    const = lambda shape: pl.BlockSpec(shape, lambda g, k: (0,) * len(shape))
    dir_spec = lambda shape: pl.BlockSpec((1,) + shape[1:], lambda g, k: (dirn,) + (0,) * (len(shape) - 1))
    in_specs = [
        seq_spec(u_blk),
        pl.BlockSpec((SUBLANES, sp2), lambda g, k: (g, 0)),
        const((rows, rows)),
        dir_spec(bmat.shape), dir_spec(cmat.shape), dir_spec(prm["abar"].shape),
    ]
    args = [z3, h0, jnp.asarray(pm, BF16), bmat, cmat, prm["abar"]]
    if final:
        yf, gate_blk, d, w_glu, b_glu = final_args
        in_specs += [const((rows, rows)), seq_spec(0), seq_spec(gate_blk),
                     const((1, wbr)), const((wbr, wbr)), const((1, wbr))]
        args += [jnp.asarray(pm.T, BF16), yf, z3, d.reshape(1, wbr).astype(F32), w_glu.astype(BF16),
                 b_glu.reshape(1, wbr).astype(F32)]
    return pl.pallas_call(
        functools.partial(_s5_kernel, reverse=reverse, final=final, tiles_per_blk=tiles_per_blk),
        grid=(bsz // SUBLANES, nchunks),
        in_specs=in_specs,
        out_specs=[seq_spec(0), pl.BlockSpec((SUBLANES, sp2), lambda g, k: (g, 0))],
        out_shape=[
            jax.ShapeDtypeStruct((bsz, n, wbr), BF16 if final else F32),
            jax.ShapeDtypeStruct((bsz, sp2), F32),
        ],
        scratch_shapes=[
            pltpu.VMEM((sp2 // lt, rows, lt), F32),
            pltpu.VMEM((rows, sp2), BF16),
            pltpu.VMEM((SUBLANES, sp2), F32),
        ],
        compiler_params=_cparams(("parallel", "arbitrary")),
        name="s5_final" if final else "s5_fwd",
    )(*args)


def _s5(z3, h0, prm, d, w_glu, b_glu, wbr, u_blk, gate_blk, tcb):
    yf, hf = _s5_pass(z3, h0[0], prm, 0, wbr, u_blk, tcb)
    out, hb = _s5_pass(z3, h0[1], prm, 1, wbr, u_blk, tcb, final_args=(yf, gate_blk, d, w_glu, b_glu))
    return out, jnp.stack([hf, hb])


def _merge_kernel(h_ref, y0, y1, y2, y3, w0, w1, w2, w3, bm_ref, u0, u1, u2, u3, o_ref):
    h = h_ref[...]
    acc = None
    for i, (y, w, u) in enumerate(zip((y0, y1, y2, y3), (w0, w1, w2, w3), (u0, u1, u2, u3))):
        term = jax.nn.sigmoid(_dot(h, w[...]) + bm_ref[i]) * _dot(y[...], u[0])
        acc = term if acc is None else acc + term
    o_ref[...] = acc.astype(BF16)


def _merge(h2, ys, w_m, b_m, w_up, tm, tn):
    t, d = h2.shape
    wbr = ys[0].shape[1]
    ns = d // tn
    y_spec = pl.BlockSpec((tm, wbr), lambda i, s: (i, 0))
    w_specs = [pl.BlockSpec((d, tn), functools.partial(lambda i, s, br: (0, br * ns + s), br=br))
               for br in range(N_BRANCH)]
    u_specs = [pl.BlockSpec((1, wbr, tn), functools.partial(lambda i, s, br: (br, 0, s), br=br))
               for br in range(N_BRANCH)]
    return pl.pallas_call(
        _merge_kernel,
        grid=(t // tm, ns),
        in_specs=[
            pl.BlockSpec((tm, d), lambda i, s: (i, 0)),
            y_spec, y_spec, y_spec, y_spec,
            *w_specs,
            pl.BlockSpec((N_BRANCH, 1, tn), lambda i, s: (0, 0, s)),
            *u_specs,
        ],
        out_specs=pl.BlockSpec((tm, tn), lambda i, s: (i, s)),
        out_shape=jax.ShapeDtypeStruct((t, d), BF16),
        compiler_params=_cparams(("parallel", "arbitrary")),
        name="merge",
    )(h2, *ys, w_m, w_m, w_m, w_m, b_m, w_up, w_up, w_up, w_up)


def _outproj_kernel(*refs, d, alpha, sub, emit_h):
    if emit_h:
        m_ref, x_ref, mod_ref, modn_ref, w_ref, g_ref, b_ref, o_ref, h_ref = refs
    else:
        m_ref, x_ref, mod_ref, w_ref, g_ref, b_ref, o_ref = refs
    gate = mod_ref[0, :, 2 * d:3 * d]
    for r in range(m_ref.shape[0] // sub):
        rows = slice(r * sub, (r + 1) * sub)
        out = _dot(m_ref[rows, :], w_ref[...])
        xn = _ln(alpha * x_ref[rows, :] + gate * out) * g_ref[...] + b_ref[...]
        o_ref[rows, :] = xn
        if emit_h:
            h_ref[rows, :] = _modulated(xn, modn_ref, d).astype(BF16)


def _out_proj(m2, x2, mod, mod_next, w_o, ln_g, ln_b, alpha, tm):
    t, d = x2.shape
    tiles_per_mod = (t // mod.shape[0]) // tm
    emit_h = mod_next is not None
    row = pl.BlockSpec((tm, d), lambda i: (i, 0))
    vec = pl.BlockSpec((1, d), lambda i: (0, 0))
    mods = [_mod_spec(d, tiles_per_mod)] * (2 if emit_h else 1)
    mod_args = (mod, mod_next) if emit_h else (mod,)
    out = pl.pallas_call(
        functools.partial(_outproj_kernel, d=d, alpha=alpha, sub=_pick_tile(tm, 128), emit_h=emit_h),
        grid=(t // tm,),
        in_specs=[row, row, *mods, pl.BlockSpec((d, d), lambda i: (0, 0)), vec, vec],
        out_specs=[row, row] if emit_h else row,
        out_shape=([jax.ShapeDtypeStruct((t, d), F32), jax.ShapeDtypeStruct((t, d), BF16)] if emit_h
                   else jax.ShapeDtypeStruct((t, d), F32)),
        compiler_params=_cparams(("parallel",)),
        name="out_proj",
    )(m2, x2, *mod_args, w_o, ln_g.reshape(1, d), ln_b.reshape(1, d))
    return out if emit_h else (out, None)


def kernel(x, c, ctx, c_ctx, w_ada, b_ada, w_in, b_in, w_sgu, b_sgu, w_fnet, b_fnet, rpb, s5_a_re, s5_a_im,
           s5_log_dt, s5_b_re, s5_b_im, s5_c_re, s5_c_im, s5_d, s5_w_glu, s5_b_glu, w_up, w_o, ln_g, ln_b):
    bsz, n, d = x.shape
    lc = ctx.shape[1]
    depth = w_ada.shape[0]
    wbr = d // N_BRANCH
    heads = rpb.shape[1]
    win_h, win_w = (rpb.shape[2] + 1) // 2, (rpb.shape[3] + 1) // 2
    rows = n // GRID_W
    alpha = (2 * depth) ** 0.25
    sp2 = 2 * s5_a_re.shape[2] * s5_a_re.shape[3]

    W = wbr
    take = lambda a, spans: jnp.concatenate([a[..., lo * W:hi * W] for lo, hi in spans], axis=-1)
    spans16 = ((1, 4), (6, 7))
    spans32 = ((4, 6), (0, 1), (7, 11))
    off_merge = 11 * W
    blk_k, blk_v, blk_q, blk_b = 0, 1, 2, 3
    blk_uv, blk_u, blk_gate = 0, 2, 3
    n16, n32 = 4 * W, 7 * W

    n_rows = -(-(bsz + 1) // 8) * 8
    cond = jnp.zeros((n_rows, d), F32).at[:bsz].set(c).at[bsz].set(c_ctx)
    ada = _ada_all(cond, w_ada, b_ada)
    mod_x = [ada[l, :bsz].reshape(bsz, 1, 3 * d) for l in range(depth)]
    mod_c = [ada[l, bsz:bsz + 1].reshape(1, 1, 3 * d) for l in range(depth)]

    w16 = take(w_in, spans16).astype(BF16)
    w32 = take(w_in, spans32).astype(BF16)
    w_m = w_in[:, :, off_merge:].astype(BF16)
    w_up_b = w_up.astype(BF16)
    w_o_b = w_o.astype(BF16)

    plan = _nattn_plan(rows, GRID_W, win_h, win_w)
    gd_f = w_fnet.shape[2]
    cc, sc = _dft_tables(gd_f)
    dft_x = _dft_tables(n) + (cc, sc)
    dft_c = _dft_tables(lc) + (cc, sc)

    tcb_x, tcb_c = _pick_tile(n, S5_ROWS), _pick_tile(lc, S5_ROWS)
    tx, tc_all = bsz * n, bsz * lc
    tm_x, tm_c = _pick_tile(n, 1024), _pick_tile(tc_all, 1024)
    tm_o_x, tm_o_c = _pick_tile(n, 512), _pick_tile(tc_all, 512)
    tn16 = _pick_tile(n16, 1024)
    tn32 = n32 // 2 if (n32 // 2) % LANES == 0 else W

    xt = x.reshape(tx, d)
    ct = ctx.reshape(tc_all, d)
    hx = _modulate(xt, mod_x[0], tm_o_x)
    hc = _modulate(ct, mod_c[0], tm_o_c)
    for l in range(depth):
        need_ctx = l < depth - 1
        b16 = take(b_in[l], spans16).reshape(1, n16)
        b32 = take(b_in[l], spans32).reshape(1, n32)
        b_m = b_in[l, off_merge:].reshape(N_BRANCH, 1, d)

        zxb = _in_proj(hx, w16[l], b16, n16, tm_x, tn16, BF16).reshape(bsz, n, n16)
        zxf = _in_proj(hx, w32[l], b32, n32, tm_x, tn32, F32).reshape(bsz, n, n32)
        c16, c32 = (n16, n32) if need_ctx else (2 * W, 3 * W)
        zcb = _in_proj(hc, w16[l], b16, c16, tm_c, _pick_tile(c16, 1024), BF16).reshape(bsz, lc, c16)
        zcf = _in_proj(hc, w32[l], b32, c32, tm_c, tn32 if need_ctx else W, F32).reshape(bsz, lc, c32)

        prm = _s5_params(s5_a_re[l], s5_a_im[l], s5_log_dt[l], s5_b_re[l], s5_b_im[l], s5_c_re[l], s5_c_im[l])
        gate_c = blk_gate + 3 if need_ctx else blk_u
        yd_c, h_ctx = _s5(zcf, jnp.zeros((2, bsz, sp2), F32), prm, s5_d[l], s5_w_glu[l], s5_b_glu[l],
                          W, blk_u, gate_c, tcb_c)
        yd_x, _ = _s5(zxf, h_ctx, prm, s5_d[l], s5_w_glu[l], s5_b_glu[l], W, blk_u, blk_gate + 3, tcb_x)

        bias = _nattn_bias(rpb[l], plan, GRID_W)
        yc_x = _nattn(zxb, zcb, zxf, bias, plan, heads, W, (blk_k, blk_v, blk_q), blk_gate + 2)
        ya_x = _sgu(zxf, w_sgu[l], b_sgu[l], W, blk_uv, blk_gate + 0, _pick_tile(n, 512))
        yb_x = _fnet(zxb, zxf, w_fnet[l], b_fnet[l], dft_x, W, blk_b, blk_gate + 1, _pick_tile(n, 512))

        ys = [y.reshape(tx, W) for y in (ya_x, yb_x, yc_x, yd_x)]
        m = _merge(hx, ys, w_m[l], b_m, w_up_b[l], tm_x, _pick_tile(d, 512))
        xt_new, hx_new = _out_proj(m, xt, mod_x[l], mod_x[l + 1] if need_ctx else None,
                                   w_o_b[l], ln_g[l], ln_b[l], alpha, tm_o_x)

        if need_ctx:
            yc_c = _cattn(zcb, zcf, heads, W, (blk_k, blk_v, blk_q), blk_gate + 2)
            ya_c = _sgu(zcf, w_sgu[l], b_sgu[l], W, blk_uv, blk_gate + 0, _pick_tile(lc, 512))
            yb_c = _fnet(zcb, zcf, w_fnet[l], b_fnet[l], dft_c, W, blk_b, blk_gate + 1, _pick_tile(lc, 512))
            ys_c = [y.reshape(tc_all, W) for y in (ya_c, yb_c, yc_c, yd_c)]
            m_c = _merge(hc, ys_c, w_m[l], b_m, w_up_b[l], tm_c, _pick_tile(d, 512))
            ct, hc = _out_proj(m_c, ct, mod_c[l], mod_c[l + 1], w_o_b[l], ln_g[l], ln_b[l], alpha, tm_o_c)
        xt, hx = xt_new, hx_new
    return xt.reshape(bsz, n, d)
```

```python
import functools
import math

import jax
import jax.numpy as jnp
import numpy as np
from jax import lax
from jax.experimental import pallas as pl
from jax.experimental.pallas import tpu as pltpu

GRID_W = 64
N_BRANCH = 4
LN_EPS = 1e-6
NEG_INF = -1e30
Q_ROWS = 4
SUBLANES = 8
LANES = 128
S5_ROWS = 64
TM_MATMUL = 1024
TM_OUT = 512
T_SGU = 512
T_FNET = 1024
TN_MERGE = 512
VMEM_LIMIT = 56 * 1024 * 1024

F32 = jnp.float32
BF16 = jnp.bfloat16
HIGHEST = lax.Precision.HIGHEST


def _cparams(sem):
    return pltpu.CompilerParams(dimension_semantics=sem, vmem_limit_bytes=VMEM_LIMIT)


def _dot(a, b):
    return jnp.dot(a, b, preferred_element_type=F32)


def _dot_t(a, b):
    return lax.dot_general(a, b, (((1,), (1,)), ((), ())), preferred_element_type=F32)


def _ln(x):
    mu = jnp.mean(x, axis=-1, keepdims=True)
    xc = x - mu
    var = jnp.mean(xc * xc, axis=-1, keepdims=True)
    return xc * lax.rsqrt(var + LN_EPS)


def _silu(x):
    return x * jax.nn.sigmoid(x)


def _pick_tile(n, target):
    t = min(n, target)
    while n % t:
        t //= 2
    return t


def _mod_spec(d, tiles_per_mod):
    return pl.BlockSpec((1, 1, 3 * d), lambda i, *_: (i // tiles_per_mod, 0, 0))


def _ada_kernel(s_ref, w_ref, b_ref, o_ref):
    s = _silu(s_ref[...]).astype(BF16)
    o_ref[0] = _dot(s, w_ref[0].astype(BF16)) + b_ref[0]


def _ada_all(cond, w_ada, b_ada):
    depth, d, d3 = w_ada.shape
    r = cond.shape[0]
    tn = _pick_tile(d3, 1024)
    return pl.pallas_call(
        _ada_kernel,
        grid=(depth, d3 // tn),
        in_specs=[
            pl.BlockSpec((r, d), lambda l, j: (0, 0)),
            pl.BlockSpec((1, d, tn), lambda l, j: (l, 0, j)),
            pl.BlockSpec((1, 1, tn), lambda l, j: (l, 0, j)),
        ],
        out_specs=pl.BlockSpec((1, r, tn), lambda l, j: (l, 0, j)),
        out_shape=jax.ShapeDtypeStruct((depth, r, d3), F32),
        compiler_params=_cparams(("arbitrary", "arbitrary")),
        name="ada_proj",
    )(cond, w_ada, b_ada.reshape(depth, 1, d3))


def _modulated(x, mod_ref, d):
    return _ln(x) * (1.0 + mod_ref[0, :, d:2 * d]) + mod_ref[0, :, 0:d]


def _modulate_kernel(x_ref, mod_ref, o_ref, *, d):
    o_ref[...] = _modulated(x_ref[...], mod_ref, d).astype(BF16)


def _modulate(x2, mod, tm):
    t, d = x2.shape
    tiles_per_mod = (t // mod.shape[0]) // tm
    return pl.pallas_call(
        functools.partial(_modulate_kernel, d=d),
        grid=(t // tm,),
        in_specs=[pl.BlockSpec((tm, d), lambda i: (i, 0)), _mod_spec(d, tiles_per_mod)],
        out_specs=pl.BlockSpec((tm, d), lambda i: (i, 0)),
        out_shape=jax.ShapeDtypeStruct((t, d), BF16),
        compiler_params=_cparams(("parallel",)),
        name="modulate",
    )(x2, mod)


def _inproj_kernel(h_ref, w_ref, b_ref, o_ref):
    o_ref[...] = (_dot(h_ref[...], w_ref[...]) + b_ref[...]).astype(o_ref.dtype)


def _in_proj(h2, w, b, ncols, tm, tn, dtype):
    t, d = h2.shape
    return pl.pallas_call(
        _inproj_kernel,
        grid=(t // tm, ncols // tn),
        in_specs=[
            pl.BlockSpec((tm, d), lambda i, j: (i, 0)),
            pl.BlockSpec((d, tn), lambda i, j: (0, j)),
            pl.BlockSpec((1, tn), lambda i, j: (0, j)),
        ],
        out_specs=pl.BlockSpec((tm, tn), lambda i, j: (i, j)),
        out_shape=jax.ShapeDtypeStruct((t, ncols), dtype),
        compiler_params=_cparams(("parallel", "arbitrary")),
        name="in_proj",
    )(h2, w, b)


def _sgu_kernel(uv_ref, g_ref, w_ref, b_ref, o_ref, *, wbr, chunk, groups):
    uv = jax.nn.gelu(uv_ref[0])
    u = uv[:, :wbr]
    v = _ln(uv[:, wbr:]).astype(BF16)
    gate = _silu(g_ref[0])
    gd = wbr // groups
    for k in range(uv.shape[0] // chunk):
        rows = slice(k * chunk, (k + 1) * chunk)
        for g in range(groups):
            cols = slice(g * gd, (g + 1) * gd)
            mixed = _dot(w_ref[g], v[rows, cols]) + b_ref[g]
            o_ref[0, rows, cols] = (u[rows, cols] * mixed * gate[rows, cols]).astype(BF16)


def _sgu(z3, w_sgu, b_sgu, wbr, uv_blk, gate_blk, ta):
    bsz, n, _ = z3.shape
    groups, chunk, _ = w_sgu.shape
    return pl.pallas_call(
        functools.partial(_sgu_kernel, wbr=wbr, chunk=chunk, groups=groups),
        grid=(bsz, n // ta),
        in_specs=[
            pl.BlockSpec((1, ta, 2 * wbr), lambda b, i: (b, i, uv_blk)),
            pl.BlockSpec((1, ta, wbr), lambda b, i: (b, i, gate_blk)),
            pl.BlockSpec((groups, chunk, chunk), lambda b, i: (0, 0, 0)),
            pl.BlockSpec((groups, chunk, 1), lambda b, i: (0, 0, 0)),
        ],
        out_specs=pl.BlockSpec((1, ta, wbr), lambda b, i: (b, i, 0)),
        out_shape=jax.ShapeDtypeStruct((bsz, n, wbr), BF16),
        compiler_params=_cparams(("parallel", "arbitrary")),
        name="sgu_branch",
    )(z3, z3, w_sgu.astype(BF16), b_sgu.reshape(groups, chunk, 1))


def _dft_tables(n):
    k = jnp.arange(n, dtype=jnp.int32)
    r = (k[:, None] * k[None, :]) % n
    ang = r.astype(F32) * (2.0 * math.pi / n)
    return jnp.cos(ang).astype(BF16), jnp.sin(ang).astype(BF16)


def _fnet_kernel(z_ref, g_ref, cc_ref, sc_ref, cn_ref, sn_ref, wf_ref, bf_ref, o_ref, p_ref, q_ref,
                 *, groups, norm):
    wbr = z_ref.shape[2]
    gd = wbr // groups

    @pl.when(pl.program_id(1) == 0)
    def _():
        for g in range(groups):
            cols = slice(g * gd, (g + 1) * gd)
            zg = z_ref[0, :, cols]
            p_ref[:, cols] = _dot(zg, cc_ref[...]).astype(BF16)
            q_ref[:, cols] = _dot(zg, sc_ref[...]).astype(BF16)

    f = (_dot(cn_ref[...], p_ref[...]) - _dot(sn_ref[...], q_ref[...])) * norm
    gate = _silu(g_ref[0])
    for g in range(groups):
        cols = slice(g * gd, (g + 1) * gd)
        y = _dot(f[:, cols].astype(BF16), wf_ref[g]) + bf_ref[:, cols]
        o_ref[0, :, cols] = (y * gate[:, cols]).astype(BF16)


def _fnet(zb3, zf3, w_f, b_f, tables, wbr, z_blk, gate_blk, tk):
    bsz, n, _ = zb3.shape
    groups, gd, _ = w_f.shape
    cn, sn, cc, sc = tables
    return pl.pallas_call(
        functools.partial(_fnet_kernel, groups=groups, norm=1.0 / math.sqrt(n * gd)),
        grid=(bsz, n // tk),
        in_specs=[
            pl.BlockSpec((1, n, wbr), lambda b, k: (b, 0, z_blk)),
            pl.BlockSpec((1, tk, wbr), lambda b, k: (b, k, gate_blk)),
            pl.BlockSpec((gd, gd), lambda b, k: (0, 0)),
            pl.BlockSpec((gd, gd), lambda b, k: (0, 0)),
            pl.BlockSpec((tk, n), lambda b, k: (k, 0)),
            pl.BlockSpec((tk, n), lambda b, k: (k, 0)),
            pl.BlockSpec((groups, gd, gd), lambda b, k: (0, 0, 0)),
            pl.BlockSpec((1, wbr), lambda b, k: (0, 0)),
        ],
        out_specs=pl.BlockSpec((1, tk, wbr), lambda b, k: (b, k, 0)),
        out_shape=jax.ShapeDtypeStruct((bsz, n, wbr), BF16),
        scratch_shapes=[pltpu.VMEM((n, wbr), BF16), pltpu.VMEM((n, wbr), BF16)],
        compiler_params=_cparams(("parallel", "arbitrary")),
        name="fnet_branch",
    )(zb3, zf3, cc, sc, cn, sn, w_f.astype(BF16), b_f.reshape(1, wbr))


def _nattn_plan(rows, width, win_h, win_w):
    kh, kw = min(win_h, rows), min(win_w, width)
    qb = Q_ROWS
    wr = min(kh + qb - 1, rows)
    nblk = rows // qb
    r0 = np.arange(nblk) * qb
    ks = np.clip(r0 - kh // 2, 0, rows - wr)
    r = r0[:, None] + np.arange(qb)[None]
    rs = np.clip(r - kh // 2, 0, rows - kh)
    kr = ks[:, None] + np.arange(wr)[None]
    row_ok = (kr[:, None, :] >= rs[:, :, None]) & (kr[:, None, :] < rs[:, :, None] + kh)
    assert row_ok.sum(-1).min() == kh
    dri = np.where(row_ok, kr[:, None, :] - r[:, :, None] + win_h - 1, 0)
    key = np.concatenate([dri.reshape(nblk, -1), row_ok.reshape(nblk, -1)], axis=1)
    _, first, inv = np.unique(key, axis=0, return_index=True, return_inverse=True)
    qc = np.arange(width)
    col_start = np.clip(qc - kw // 2, 0, width - kw)
    in_win = (qc[None, :] >= col_start[:, None]) & (qc[None, :] < col_start[:, None] + kw)
    dci = np.clip(qc[None, :] - qc[:, None], -(kw - 1), kw - 1) + win_w - 1
    return dict(qb=qb, wr=wr, nblk=nblk, ks=ks.astype(np.int32), pat=inv.reshape(-1).astype(np.int32),
                dri=dri[first], row_ok=row_ok[first], in_win=in_win, dci=dci)


def _nattn_bias(rpb, plan, width):
    dri, row_ok = plan["dri"], plan["row_ok"]
    npat, qb, wr = dri.shape
    heads, _, n_dc = rpb.shape
    rows = jnp.take(rpb.astype(F32), jnp.asarray(dri.reshape(-1)), axis=1)
    onehot = (plan["dci"].reshape(-1)[None, :] == np.arange(n_dc)[:, None]).astype(np.float32)
    vals = jnp.einsum("hrc,cw->hrw", rows, jnp.asarray(onehot), precision=HIGHEST)
    vals = vals.reshape(heads, npat, qb, wr, width, width).transpose(1, 0, 2, 4, 3, 5)
    valid = row_ok[:, None, :, None, :, None] & plan["in_win"][None, None, None, :, None, :]
    vals = jnp.where(jnp.asarray(valid), vals, NEG_INF)
    return vals.reshape(npat, heads, qb * width, wr * width)


def _softmax_pv(s_band, s_ctx, v_band, v_ctx):
    m = jnp.maximum(jnp.max(s_band, axis=-1, keepdims=True), jnp.max(s_ctx, axis=-1, keepdims=True))
    p_band = jnp.exp(s_band - m)
    p_ctx = jnp.exp(s_ctx - m)
    den = jnp.sum(p_band, axis=-1, keepdims=True) + jnp.sum(p_ctx, axis=-1, keepdims=True)
    out = _dot(p_band.astype(BF16), v_band) + _dot(p_ctx.astype(BF16), v_ctx)
    return out / den


def _nattn_kernel(ks_ref, pat_ref, q_ref, k_ref, v_ref, ck_ref, cv_ref, g_ref, bias_ref, o_ref,
                  *, heads, width, nkeys):
    blk = pl.program_id(1)
    start = pl.multiple_of(ks_ref[blk] * width, width)
    pat = pat_ref[blk]
    dh = q_ref.shape[2] // heads
    scale = dh ** -0.5
    gate = _silu(g_ref[0])
    for h in range(heads):
        cols = slice(h * dh, (h + 1) * dh)
        kb = k_ref[0, pl.ds(start, nkeys), cols]
        vb = v_ref[0, pl.ds(start, nkeys), cols]
        s_band = _dot_t(q_ref[0, :, cols], kb) * scale + bias_ref[pat, h]
        s_ctx = _dot_t(q_ref[0, :, cols], ck_ref[0, :, cols]) * scale
        out = _softmax_pv(s_band, s_ctx, vb, cv_ref[0, :, cols])
        o_ref[0, :, cols] = (out * gate[:, cols]).astype(BF16)


def _nattn(zxb, zcb, zxf, bias, plan, heads, wbr, blks, gate_blk):
    bsz, n, _ = zxb.shape
    lc = zcb.shape[1]
    k_blk, v_blk, q_blk = blks
    nq = plan["qb"] * GRID_W
    nkeys = plan["wr"] * GRID_W
    grid_spec = pltpu.PrefetchScalarGridSpec(
        num_scalar_prefetch=2,
        grid=(bsz, plan["nblk"]),
        in_specs=[
            pl.BlockSpec((1, nq, wbr), lambda b, i, *_: (b, i, q_blk)),
            pl.BlockSpec((1, n, wbr), lambda b, i, *_: (b, 0, k_blk)),
            pl.BlockSpec((1, n, wbr), lambda b, i, *_: (b, 0, v_blk)),
            pl.BlockSpec((1, lc, wbr), lambda b, i, *_: (b, 0, k_blk)),
            pl.BlockSpec((1, lc, wbr), lambda b, i, *_: (b, 0, v_blk)),
            pl.BlockSpec((1, nq, wbr), lambda b, i, *_: (b, i, gate_blk)),
            pl.BlockSpec(bias.shape, lambda b, i, *_: (0, 0, 0, 0)),
        ],
        out_specs=pl.BlockSpec((1, nq, wbr), lambda b, i, *_: (b, i, 0)),
    )
    return pl.pallas_call(
        functools.partial(_nattn_kernel, heads=heads, width=GRID_W, nkeys=nkeys),
        grid_spec=grid_spec,
        out_shape=jax.ShapeDtypeStruct((bsz, n, wbr), BF16),
        compiler_params=_cparams(("parallel", "arbitrary")),
        name="nattn_branch",
    )(jnp.asarray(plan["ks"]), jnp.asarray(plan["pat"]), zxb, zxb, zxb, zcb, zcb, zxf, bias)


def _cattn_kernel(q_ref, k_ref, v_ref, g_ref, o_ref, *, heads):
    dh = q_ref.shape[2] // heads
    scale = dh ** -0.5
    gate = _silu(g_ref[0])
    for h in range(heads):
        cols = slice(h * dh, (h + 1) * dh)
        s = _dot_t(q_ref[0, :, cols], k_ref[0, :, cols]) * scale
        p = jnp.exp(s - jnp.max(s, axis=-1, keepdims=True))
        out = _dot(p.astype(BF16), v_ref[0, :, cols]) / jnp.sum(p, axis=-1, keepdims=True)
        o_ref[0, :, cols] = (out * gate[:, cols]).astype(BF16)


def _cattn(zcb, zcf, heads, wbr, blks, gate_blk):
    bsz, lc, _ = zcb.shape
    k_blk, v_blk, q_blk = blks
    spec = lambda blk: pl.BlockSpec((1, lc, wbr), lambda b: (b, 0, blk))
    return pl.pallas_call(
        functools.partial(_cattn_kernel, heads=heads),
        grid=(bsz,),
        in_specs=[spec(q_blk), spec(k_blk), spec(v_blk), spec(gate_blk)],
        out_specs=spec(0),
        out_shape=jax.ShapeDtypeStruct((bsz, lc, wbr), BF16),
        compiler_params=_cparams(("parallel",)),
        name="cattn_branch",
    )(zcb, zcb, zcb, zcf)


def _s5_discretise(a_re, a_im, log_dt, b_re, b_im):
    dt = jnp.exp(log_dt)[..., None]
    lr, li = a_re * dt, a_im * dt
    mag = jnp.exp(lr)
    abar_re, abar_im = mag * jnp.cos(li), mag * jnp.sin(li)
    den = a_re * a_re + a_im * a_im
    f_re = ((abar_re - 1.0) * a_re + abar_im * a_im) / den
    f_im = (abar_im * a_re - (abar_re - 1.0) * a_im) / den
    bb_re = f_re[..., None] * b_re - f_im[..., None] * b_im
    bb_im = f_re[..., None] * b_im + f_im[..., None] * b_re
    return abar_re, abar_im, bb_re, bb_im


def _block_diag(m, gpb):
    two, g, a, b = m.shape
    nb = g // gpb
    m = m.reshape(two, nb, gpb, a, b)
    eye = jnp.eye(gpb, dtype=m.dtype)
    out = m[:, :, :, :, None, :] * eye[None, None, :, None, :, None]
    return out.reshape(two, nb, gpb * a, gpb * b)


def _s5_params(a_re, a_im, log_dt, b_re, b_im, c_re, c_im):
    f32 = lambda t: t.astype(F32)
    a_re, a_im, log_dt, b_re, b_im, c_re, c_im = map(f32, (a_re, a_im, log_dt, b_re, b_im, c_re, c_im))
    _, g, p, h = b_re.shape
    gpb = min(g, max(1, LANES // h))
    abar_re, abar_im, bb_re, bb_im = _s5_discretise(a_re, a_im, log_dt, b_re, b_im)
    tr = lambda t: jnp.swapaxes(t, -1, -2)
    bmat = jnp.concatenate([_block_diag(tr(bb_re), gpb), _block_diag(tr(bb_im), gpb)], axis=-1)
    cmat = jnp.stack([_block_diag(tr(c_re), gpb), _block_diag(-tr(c_im), gpb)], axis=2)
    abar = jnp.concatenate([abar_re.reshape(2, 1, g * p), abar_im.reshape(2, 1, g * p)], axis=-1)
    return dict(bmat=bmat.astype(BF16), cmat=cmat.astype(BF16), abar=abar)


def _interleave_perm(nseq, rows):
    src = np.arange(nseq * rows)
    dst = (src % rows) * nseq + src // rows
    pm = np.zeros((nseq * rows, nseq * rows), np.float32)
    pm[dst, src] = 1.0
    return pm


def _cmul_add(ar, ai, xr, xi, br, bi):
    return ar * xr - ai * xi + br, ar * xi + ai * xr + bi


def _permute_rows_exact(pm, y):
    hi = y.astype(BF16)
    r1 = y - hi.astype(F32)
    mid = r1.astype(BF16)
    lo = (r1 - mid.astype(F32)).astype(BF16)
    return _dot(pm, hi) + _dot(pm, mid) + _dot(pm, lo)


def _s5_kernel(*refs, reverse, final, tiles_per_blk):
    if final:
        (u_ref, h0_ref, pm_ref, bm_ref, cm_ref, a_ref, pmt_ref, yf_ref, g_ref, d_ref, wg_ref, bg_ref,
         o_ref, ht_ref, xs_ref, xb_ref, st_ref) = refs
    else:
        (u_ref, h0_ref, pm_ref, bm_ref, cm_ref, a_ref,
         o_ref, ht_ref, xs_ref, xb_ref, st_ref) = refs
    nseq, tcb, wbr = u_ref.shape
    n_tiles, lt = xs_ref.shape[0] // 2, xs_ref.shape[2]
    sp = n_tiles * lt
    nb = bm_ref.shape[1]
    bw_in, bw_st = wbr // nb, sp // nb
    tpb = bw_st // lt
    re_cols = lambda t: slice(t * lt, (t + 1) * lt)
    im_cols = lambda t: slice(sp + t * lt, sp + (t + 1) * lt)

    @pl.when(pl.program_id(1) == 0)
    def _():
        st_ref[...] = h0_ref[...]

    u = u_ref[...].reshape(nseq * tcb, wbr)
    ub = _dot(pm_ref[...], u.astype(BF16)).astype(BF16)
    for n in range(nb):
        bu = _dot(ub[:, n * bw_in:(n + 1) * bw_in], bm_ref[0, n])
        for t in range(tpb):
            xs_ref[n * tpb + t] = bu[:, t * lt:(t + 1) * lt]
            xs_ref[n_tiles + n * tpb + t] = bu[:, bw_st + t * lt:bw_st + (t + 1) * lt]

    npairs = tcb // 2
    for q in range(n_tiles // tiles_per_blk):
        tiles = range(q * tiles_per_blk, (q + 1) * tiles_per_blk)
        ar = [jnp.broadcast_to(a_ref[0, :, re_cols(t)], (nseq, lt)) for t in tiles]
        ai = [jnp.broadcast_to(a_ref[0, :, im_cols(t)], (nseq, lt)) for t in tiles]

        def step(i, st):
            jj = npairs - 1 - i if reverse else i
            r0 = pl.multiple_of(jj * 2 * nseq, 2 * nseq)
            out = []
            for idx, t in enumerate(tiles):
                dr, di = xs_ref[t, pl.ds(r0, 2 * nseq), :], xs_ref[n_tiles + t, pl.ds(r0, 2 * nseq), :]
                first, second = (slice(nseq, None), slice(0, nseq)) if reverse else (slice(0, nseq), slice(nseq, None))
                x0 = _cmul_add(ar[idx], ai[idx], st[2 * idx], st[2 * idx + 1], dr[first], di[first])
                x1 = _cmul_add(ar[idx], ai[idx], x0[0], x0[1], dr[second], di[second])
                lo, hi = (x1, x0) if reverse else (x0, x1)
                xb_ref[pl.ds(r0, 2 * nseq), re_cols(t)] = jnp.concatenate([lo[0], hi[0]], axis=0).astype(BF16)
                xb_ref[pl.ds(r0, 2 * nseq), im_cols(t)] = jnp.concatenate([lo[1], hi[1]], axis=0).astype(BF16)
                out += [x1[0], x1[1]]
            return tuple(out)

        init = []
        for t in tiles:
            init += [st_ref[:, re_cols(t)], st_ref[:, im_cols(t)]]
        last = lax.fori_loop(0, npairs, step, tuple(init), unroll=4)
        for idx, t in enumerate(tiles):
            st_ref[:, re_cols(t)] = last[2 * idx]
            st_ref[:, im_cols(t)] = last[2 * idx + 1]

    ht_ref[...] = st_ref[...]

    parts = []
    for n in range(nb):
        parts.append(_dot(xb_ref[:, n * bw_st:(n + 1) * bw_st], cm_ref[0, n, 0])
                     + _dot(xb_ref[:, sp + n * bw_st:sp + (n + 1) * bw_st], cm_ref[0, n, 1]))
    y = parts[0] if nb == 1 else jnp.concatenate(parts, axis=-1)

    if final:
        y = _permute_rows_exact(pmt_ref[...], yf_ref[...].reshape(nseq * tcb, wbr) + y)
        tot = jax.nn.gelu(y + d_ref[...] * u)
        glu = jax.nn.sigmoid(_dot(tot.astype(BF16), wg_ref[...]) + bg_ref[...])
        gate = _silu(g_ref[...].reshape(nseq * tcb, wbr))
        o_ref[...] = (tot * glu * gate).astype(BF16).reshape(nseq, tcb, wbr)
    else:
        o_ref[...] = y.reshape(nseq, tcb, wbr)


def _s5_pass(z3, h0, prm, dirn, wbr, u_blk, tcb, final_args=None):
    bsz, n, _ = z3.shape
    assert bsz % SUBLANES == 0 and n % tcb == 0 and tcb % 2 == 0
    nchunks = n // tcb
    reverse = dirn == 1
    final = final_args is not None
    sp2 = prm["abar"].shape[-1]
    bmat, cmat = prm["bmat"], prm["cmat"]
    sp = sp2 // 2
    lt = min(LANES, sp // bmat.shape[1])
    tiles_per_blk = min(sp // lt, 4)
    rows = SUBLANES * tcb
    pm = _interleave_perm(SUBLANES, tcb)
    chunk = (lambda k: nchunks - 1 - k) if reverse else (lambda k: k)
    seq_spec = lambda blk: pl.BlockSpec((SUBLANES, tcb, wbr), lambda g, k: (g, chunk(k), blk))
    const = lambda shape: pl.BlockSpec(shape, lambda g, k: (0,) * len(shape))
    dir_spec = lambda shape: pl.BlockSpec((1,) + shape[1:], lambda g, k: (dirn,) + (0,) * (len(shape) - 1))
    in_specs = [
        seq_spec(u_blk),
        pl.BlockSpec((SUBLANES, sp2), lambda g, k: (g, 0)),
        const((rows, rows)),
        dir_spec(bmat.shape), dir_spec(cmat.shape), dir_spec(prm["abar"].shape),
    ]
    args = [z3, h0, jnp.asarray(pm, BF16), bmat, cmat, prm["abar"]]
    if final:
        yf, gate_blk, d, w_glu, b_glu = final_args
        in_specs += [const((rows, rows)), seq_spec(0), seq_spec(gate_blk),
                     const((1, wbr)), const((wbr, wbr)), const((1, wbr))]
        args += [jnp.asarray(pm.T, BF16), yf, z3, d.reshape(1, wbr).astype(F32), w_glu.astype(BF16),
                 b_glu.reshape(1, wbr).astype(F32)]
    return pl.pallas_call(
        functools.partial(_s5_kernel, reverse=reverse, final=final, tiles_per_blk=tiles_per_blk),
        grid=(bsz // SUBLANES, nchunks),
        in_specs=in_specs,
        out_specs=[seq_spec(0), pl.BlockSpec((SUBLANES, sp2), lambda g, k: (g, 0))],
        out_shape=[
            jax.ShapeDtypeStruct((bsz, n, wbr), BF16 if final else F32),
            jax.ShapeDtypeStruct((bsz, sp2), F32),
        ],
        scratch_shapes=[
            pltpu.VMEM((sp2 // lt, rows, lt), F32),
            pltpu.VMEM((rows, sp2), BF16),
            pltpu.VMEM((SUBLANES, sp2), F32),
        ],
        compiler_params=_cparams(("parallel", "arbitrary")),
        name="s5_final" if final else "s5_fwd",
    )(*args)


def _s5(z3, h0, prm, d, w_glu, b_glu, wbr, u_blk, gate_blk, tcb):
    yf, hf = _s5_pass(z3, h0[0], prm, 0, wbr, u_blk, tcb)
    out, hb = _s5_pass(z3, h0[1], prm, 1, wbr, u_blk, tcb, final_args=(yf, gate_blk, d, w_glu, b_glu))
    return out, jnp.stack([hf, hb])


def _merge_kernel(h_ref, y0, y1, y2, y3, w0, w1, w2, w3, bm_ref, u0, u1, u2, u3, o_ref):
    h = h_ref[...]
    acc = None
    for i, (y, w, u) in enumerate(zip((y0, y1, y2, y3), (w0, w1, w2, w3), (u0, u1, u2, u3))):
        term = jax.nn.sigmoid(_dot(h, w[...]) + bm_ref[i]) * _dot(y[...], u[0])
        acc = term if acc is None else acc + term
    o_ref[...] = acc.astype(BF16)


def _merge(h2, ys, w_m, b_m, w_up, tm, tn):
    t, d = h2.shape
    wbr = ys[0].shape[1]
    ns = d // tn
    y_spec = pl.BlockSpec((tm, wbr), lambda i, s: (i, 0))
    w_specs = [pl.BlockSpec((d, tn), functools.partial(lambda i, s, br: (0, br * ns + s), br=br))
               for br in range(N_BRANCH)]
    u_specs = [pl.BlockSpec((1, wbr, tn), functools.partial(lambda i, s, br: (br, 0, s), br=br))
               for br in range(N_BRANCH)]
    return pl.pallas_call(
        _merge_kernel,
        grid=(t // tm, ns),
        in_specs=[
            pl.BlockSpec((tm, d), lambda i, s: (i, 0)),
            y_spec, y_spec, y_spec, y_spec,
            *w_specs,
            pl.BlockSpec((N_BRANCH, 1, tn), lambda i, s: (0, 0, s)),
            *u_specs,
        ],
        out_specs=pl.BlockSpec((tm, tn), lambda i, s: (i, s)),
        out_shape=jax.ShapeDtypeStruct((t, d), BF16),
        compiler_params=_cparams(("parallel", "arbitrary")),
        name="merge",
    )(h2, *ys, w_m, w_m, w_m, w_m, b_m, w_up, w_up, w_up, w_up)


def _outproj_kernel(*refs, d, alpha, sub, emit_h):
    if emit_h:
        m_ref, x_ref, mod_ref, modn_ref, w_ref, g_ref, b_ref, o_ref, h_ref = refs
    else:
        m_ref, x_ref, mod_ref, w_ref, g_ref, b_ref, o_ref = refs
    gate = mod_ref[0, :, 2 * d:3 * d]
    for r in range(m_ref.shape[0] // sub):
        rows = slice(r * sub, (r + 1) * sub)
        out = _dot(m_ref[rows, :], w_ref[...])
        xn = _ln(alpha * x_ref[rows, :] + gate * out) * g_ref[...] + b_ref[...]
        o_ref[rows, :] = xn
        if emit_h:
            h_ref[rows, :] = _modulated(xn, modn_ref, d).astype(BF16)


def _out_proj(m2, x2, mod, mod_next, w_o, ln_g, ln_b, alpha, tm):
    t, d = x2.shape
    tiles_per_mod = (t // mod.shape[0]) // tm
    emit_h = mod_next is not None
    row = pl.BlockSpec((tm, d), lambda i: (i, 0))
    vec = pl.BlockSpec((1, d), lambda i: (0, 0))
    mods = [_mod_spec(d, tiles_per_mod)] * (2 if emit_h else 1)
    mod_args = (mod, mod_next) if emit_h else (mod,)
    out = pl.pallas_call(
        functools.partial(_outproj_kernel, d=d, alpha=alpha, sub=_pick_tile(tm, 128), emit_h=emit_h),
        grid=(t // tm,),
        in_specs=[row, row, *mods, pl.BlockSpec((d, d), lambda i: (0, 0)), vec, vec],
        out_specs=[row, row] if emit_h else row,
        out_shape=([jax.ShapeDtypeStruct((t, d), F32), jax.ShapeDtypeStruct((t, d), BF16)] if emit_h
                   else jax.ShapeDtypeStruct((t, d), F32)),
        compiler_params=_cparams(("parallel",)),
        name="out_proj",
    )(m2, x2, *mod_args, w_o, ln_g.reshape(1, d), ln_b.reshape(1, d))
    return out if emit_h else (out, None)


def kernel(x, c, ctx, c_ctx, w_ada, b_ada, w_in, b_in, w_sgu, b_sgu, w_fnet, b_fnet, rpb, s5_a_re, s5_a_im,
           s5_log_dt, s5_b_re, s5_b_im, s5_c_re, s5_c_im, s5_d, s5_w_glu, s5_b_glu, w_up, w_o, ln_g, ln_b):
    bsz, n, d = x.shape
    lc = ctx.shape[1]
    depth = w_ada.shape[0]
    wbr = d // N_BRANCH
    heads = rpb.shape[1]
    win_h, win_w = (rpb.shape[2] + 1) // 2, (rpb.shape[3] + 1) // 2
    rows = n // GRID_W
    alpha = (2 * depth) ** 0.25
    sp2 = 2 * s5_a_re.shape[2] * s5_a_re.shape[3]

    W = wbr
    take = lambda a, spans: jnp.concatenate([a[..., lo * W:hi * W] for lo, hi in spans], axis=-1)
    spans16 = ((1, 4), (6, 7))
    spans32 = ((4, 6), (0, 1), (7, 11))
    off_merge = 11 * W
    blk_k, blk_v, blk_q, blk_b = 0, 1, 2, 3
    blk_uv, blk_u, blk_gate = 0, 2, 3
    n16, n32 = 4 * W, 7 * W

    n_rows = -(-(bsz + 1) // 8) * 8
    cond = jnp.zeros((n_rows, d), F32).at[:bsz].set(c).at[bsz].set(c_ctx)
    ada = _ada_all(cond, w_ada, b_ada)
    mod_x = [ada[l, :bsz].reshape(bsz, 1, 3 * d) for l in range(depth)]
    mod_c = [ada[l, bsz:bsz + 1].reshape(1, 1, 3 * d) for l in range(depth)]

    w16 = take(w_in, spans16).astype(BF16)
    w32 = take(w_in, spans32).astype(BF16)
    w_m = w_in[:, :, off_merge:].astype(BF16)
    w_up_b = w_up.astype(BF16)
    w_o_b = w_o.astype(BF16)

    plan = _nattn_plan(rows, GRID_W, win_h, win_w)
    gd_f = w_fnet.shape[2]
    cc, sc = _dft_tables(gd_f)
    dft_x = _dft_tables(n) + (cc, sc)
    dft_c = _dft_tables(lc) + (cc, sc)

    tcb_x, tcb_c = _pick_tile(n, S5_ROWS), _pick_tile(lc, S5_ROWS)
    tx, tc_all = bsz * n, bsz * lc
    tm_x, tm_c = _pick_tile(n, TM_MATMUL), _pick_tile(tc_all, TM_MATMUL)
    tm_o_x, tm_o_c = _pick_tile(n, TM_OUT), _pick_tile(tc_all, TM_OUT)
    tn32 = n32 // 2 if (n32 // 2) % LANES == 0 else W

    xt = x.reshape(tx, d)
    ct = ctx.reshape(tc_all, d)
    hx = _modulate(xt, mod_x[0], tm_o_x)
    hc = _modulate(ct, mod_c[0], tm_o_c)
    for l in range(depth):
        need_ctx = l < depth - 1
        b16 = take(b_in[l], spans16).reshape(1, n16)
        b32 = take(b_in[l], spans32).reshape(1, n32)
        b_m = b_in[l, off_merge:].reshape(N_BRANCH, 1, d)

        zxb = _in_proj(hx, w16[l], b16, n16, tm_x, n16, BF16).reshape(bsz, n, n16)
        zxf = _in_proj(hx, w32[l], b32, n32, tm_x, tn32, F32).reshape(bsz, n, n32)
        c16, c32 = (n16, n32) if need_ctx else (2 * W, 3 * W)
        zcb = _in_proj(hc, w16[l], b16, c16, tm_c, c16, BF16).reshape(bsz, lc, c16)
        zcf = _in_proj(hc, w32[l], b32, c32, tm_c, tn32 if need_ctx else W, F32).reshape(bsz, lc, c32)

        prm = _s5_params(s5_a_re[l], s5_a_im[l], s5_log_dt[l], s5_b_re[l], s5_b_im[l], s5_c_re[l], s5_c_im[l])
        gate_c = blk_gate + 3 if need_ctx else blk_u
        yd_c, h_ctx = _s5(zcf, jnp.zeros((2, bsz, sp2), F32), prm, s5_d[l], s5_w_glu[l], s5_b_glu[l],
                          W, blk_u, gate_c, tcb_c)
        yd_x, _ = _s5(zxf, h_ctx, prm, s5_d[l], s5_w_glu[l], s5_b_glu[l], W, blk_u, blk_gate + 3, tcb_x)

        bias = _nattn_bias(rpb[l], plan, GRID_W)
        yc_x = _nattn(zxb, zcb, zxf, bias, plan, heads, W, (blk_k, blk_v, blk_q), blk_gate + 2)
        ya_x = _sgu(zxf, w_sgu[l], b_sgu[l], W, blk_uv, blk_gate + 0, _pick_tile(n, T_SGU))
        yb_x = _fnet(zxb, zxf, w_fnet[l], b_fnet[l], dft_x, W, blk_b, blk_gate + 1, _pick_tile(n, T_FNET))

        ys = [y.reshape(tx, W) for y in (ya_x, yb_x, yc_x, yd_x)]
        m = _merge(hx, ys, w_m[l], b_m, w_up_b[l], tm_x, _pick_tile(d, TN_MERGE))
        xt_new, hx_new = _out_proj(m, xt, mod_x[l], mod_x[l + 1] if need_ctx else None,
                                   w_o_b[l], ln_g[l], ln_b[l], alpha, tm_o_x)

        if need_ctx:
            yc_c = _cattn(zcb, zcf, heads, W, (blk_k, blk_v, blk_q), blk_gate + 2)
            ya_c = _sgu(zcf, w_sgu[l], b_sgu[l], W, blk_uv, blk_gate + 0, _pick_tile(lc, T_SGU))
            yb_c = _fnet(zcb, zcf, w_fnet[l], b_fnet[l], dft_c, W, blk_b, blk_gate + 1, _pick_tile(lc, T_FNET))
            ys_c = [y.reshape(tc_all, W) for y in (ya_c, yb_c, yc_c, yd_c)]
            m_c = _merge(hc, ys_c, w_m[l], b_m, w_up_b[l], tm_c, _pick_tile(d, TN_MERGE))
            ct, hc = _out_proj(m_c, ct, mod_c[l], mod_c[l + 1], w_o_b[l], ln_g[l], ln_b[l], alpha, tm_o_c)
        xt, hx = xt_new, hx_new
    return xt.reshape(bsz, n, d)
```

```python
import functools
import math

import jax
import jax.numpy as jnp
import numpy as np
from jax import lax
from jax.experimental import pallas as pl
from jax.experimental.pallas import tpu as pltpu

GRID_W = 64
N_BRANCH = 4
LN_EPS = 1e-6
NEG_INF = -1e30
Q_ROWS = 4
SUBLANES = 8
LANES = 128
S5_ROWS = 64
TM_MATMUL = 1024
TM_OUT = 512
T_SGU = 512
T_FNET = 1024
TN_MERGE = 512
VMEM_LIMIT = 56 * 1024 * 1024

F32 = jnp.float32
BF16 = jnp.bfloat16
HIGHEST = lax.Precision.HIGHEST


def _cparams(sem):
    return pltpu.CompilerParams(dimension_semantics=sem, vmem_limit_bytes=VMEM_LIMIT)


def _dot(a, b):
    return jnp.dot(a, b, preferred_element_type=F32)


def _dot_t(a, b):
    return lax.dot_general(a, b, (((1,), (1,)), ((), ())), preferred_element_type=F32)


def _ln(x):
    mu = jnp.mean(x, axis=-1, keepdims=True)
    xc = x - mu
    var = jnp.mean(xc * xc, axis=-1, keepdims=True)
    return xc * lax.rsqrt(var + LN_EPS)


def _silu(x):
    return x * jax.nn.sigmoid(x)


def _pick_tile(n, target):
    t = min(n, target)
    while n % t:
        t //= 2
    return t


def _mod_spec(d, tiles_per_mod):
    return pl.BlockSpec((1, 1, 3 * d), lambda i, *_: (i // tiles_per_mod, 0, 0))


def _ada_kernel(s_ref, w_ref, b_ref, o_ref):
    s = _silu(s_ref[...]).astype(BF16)
    o_ref[0] = _dot(s, w_ref[0].astype(BF16)) + b_ref[0]


def _ada_all(cond, w_ada, b_ada):
    depth, d, d3 = w_ada.shape
    r = cond.shape[0]
    tn = _pick_tile(d3, 1024)
    return pl.pallas_call(
        _ada_kernel,
        grid=(depth, d3 // tn),
        in_specs=[
            pl.BlockSpec((r, d), lambda l, j: (0, 0)),
            pl.BlockSpec((1, d, tn), lambda l, j: (l, 0, j)),
            pl.BlockSpec((1, 1, tn), lambda l, j: (l, 0, j)),
        ],
        out_specs=pl.BlockSpec((1, r, tn), lambda l, j: (l, 0, j)),
        out_shape=jax.ShapeDtypeStruct((depth, r, d3), F32),
        compiler_params=_cparams(("arbitrary", "arbitrary")),
        name="ada_proj",
    )(cond, w_ada, b_ada.reshape(depth, 1, d3))


def _modulated(x, mod_ref, d):
    return _ln(x) * (1.0 + mod_ref[0, :, d:2 * d]) + mod_ref[0, :, 0:d]


def _modulate_kernel(x_ref, mod_ref, o_ref, *, d):
    o_ref[...] = _modulated(x_ref[...], mod_ref, d).astype(BF16)


def _modulate(x2, mod, tm):
    t, d = x2.shape
    tiles_per_mod = (t // mod.shape[0]) // tm
    return pl.pallas_call(
        functools.partial(_modulate_kernel, d=d),
        grid=(t // tm,),
        in_specs=[pl.BlockSpec((tm, d), lambda i: (i, 0)), _mod_spec(d, tiles_per_mod)],
        out_specs=pl.BlockSpec((tm, d), lambda i: (i, 0)),
        out_shape=jax.ShapeDtypeStruct((t, d), BF16),
        compiler_params=_cparams(("parallel",)),
        name="modulate",
    )(x2, mod)


def _modproj_kernel(x_ref, mod_ref, w_ref, b_ref, o_ref, h_ref, *, d, sub):
    for r in range(x_ref.shape[0] // sub):
        rows = slice(r * sub, (r + 1) * sub)
        h = _modulated(x_ref[rows, :], mod_ref, d).astype(BF16)
        h_ref[rows, :] = h
        o_ref[rows, :] = (_dot(h, w_ref[...]) + b_ref[...]).astype(o_ref.dtype)


def _mod_proj(x2, mod, w, b, ncols, tm, dtype):
    t, d = x2.shape
    tiles_per_mod = (t // mod.shape[0]) // tm
    return pl.pallas_call(
        functools.partial(_modproj_kernel, d=d, sub=_pick_tile(tm, 128)),
        grid=(t // tm,),
        in_specs=[
            pl.BlockSpec((tm, d), lambda i: (i, 0)),
            _mod_spec(d, tiles_per_mod),
            pl.BlockSpec((d, ncols), lambda i: (0, 0)),
            pl.BlockSpec((1, ncols), lambda i: (0, 0)),
        ],
        out_specs=[pl.BlockSpec((tm, ncols), lambda i: (i, 0)), pl.BlockSpec((tm, d), lambda i: (i, 0))],
        out_shape=[jax.ShapeDtypeStruct((t, ncols), dtype), jax.ShapeDtypeStruct((t, d), BF16)],
        compiler_params=_cparams(("parallel",)),
        name="mod_proj",
    )(x2, mod, w, b)


def _inproj_kernel(h_ref, w_ref, b_ref, o_ref):
    o_ref[...] = (_dot(h_ref[...], w_ref[...]) + b_ref[...]).astype(o_ref.dtype)


def _in_proj(h2, w, b, ncols, tm, tn, dtype):
    t, d = h2.shape
    return pl.pallas_call(
        _inproj_kernel,
        grid=(t // tm, ncols // tn),
        in_specs=[
            pl.BlockSpec((tm, d), lambda i, j: (i, 0)),
            pl.BlockSpec((d, tn), lambda i, j: (0, j)),
            pl.BlockSpec((1, tn), lambda i, j: (0, j)),
        ],
        out_specs=pl.BlockSpec((tm, tn), lambda i, j: (i, j)),
        out_shape=jax.ShapeDtypeStruct((t, ncols), dtype),
        compiler_params=_cparams(("parallel", "arbitrary")),
        name="in_proj",
    )(h2, w, b)


def _sgu_kernel(uv_ref, g_ref, w_ref, b_ref, o_ref, *, wbr, chunk, groups):
    uv = jax.nn.gelu(uv_ref[0])
    u = uv[:, :wbr]
    v = _ln(uv[:, wbr:]).astype(BF16)
    gate = _silu(g_ref[0])
    gd = wbr // groups
    for k in range(uv.shape[0] // chunk):
        rows = slice(k * chunk, (k + 1) * chunk)
        for g in range(groups):
            cols = slice(g * gd, (g + 1) * gd)
            mixed = _dot(w_ref[g], v[rows, cols]) + b_ref[g]
            o_ref[0, rows, cols] = (u[rows, cols] * mixed * gate[rows, cols]).astype(BF16)


def _sgu(z3, w_sgu, b_sgu, wbr, uv_blk, gate_blk, ta):
    bsz, n, _ = z3.shape
    groups, chunk, _ = w_sgu.shape
    return pl.pallas_call(
        functools.partial(_sgu_kernel, wbr=wbr, chunk=chunk, groups=groups),
        grid=(bsz, n // ta),
        in_specs=[
            pl.BlockSpec((1, ta, 2 * wbr), lambda b, i: (b, i, uv_blk)),
            pl.BlockSpec((1, ta, wbr), lambda b, i: (b, i, gate_blk)),
            pl.BlockSpec((groups, chunk, chunk), lambda b, i: (0, 0, 0)),
            pl.BlockSpec((groups, chunk, 1), lambda b, i: (0, 0, 0)),
        ],
        out_specs=pl.BlockSpec((1, ta, wbr), lambda b, i: (b, i, 0)),
        out_shape=jax.ShapeDtypeStruct((bsz, n, wbr), BF16),
        compiler_params=_cparams(("parallel", "arbitrary")),
        name="sgu_branch",
    )(z3, z3, w_sgu.astype(BF16), b_sgu.reshape(groups, chunk, 1))


def _dft_tables(n):
    k = jnp.arange(n, dtype=jnp.int32)
    r = (k[:, None] * k[None, :]) % n
    ang = r.astype(F32) * (2.0 * math.pi / n)
    return jnp.cos(ang).astype(BF16), jnp.sin(ang).astype(BF16)


def _fnet_kernel(z_ref, g_ref, cc_ref, sc_ref, cn_ref, sn_ref, wf_ref, bf_ref, o_ref, p_ref, q_ref,
                 *, groups, norm):
    wbr = z_ref.shape[2]
    gd = wbr // groups

    @pl.when(pl.program_id(1) == 0)
    def _():
        for g in range(groups):
            cols = slice(g * gd, (g + 1) * gd)
            zg = z_ref[0, :, cols]
            p_ref[:, cols] = _dot(zg, cc_ref[...]).astype(BF16)
            q_ref[:, cols] = _dot(zg, sc_ref[...]).astype(BF16)

    f = (_dot(cn_ref[...], p_ref[...]) - _dot(sn_ref[...], q_ref[...])) * norm
    gate = _silu(g_ref[0])
    for g in range(groups):
        cols = slice(g * gd, (g + 1) * gd)
        y = _dot(f[:, cols].astype(BF16), wf_ref[g]) + bf_ref[:, cols]
        o_ref[0, :, cols] = (y * gate[:, cols]).astype(BF16)


def _fnet(zb3, zf3, w_f, b_f, tables, wbr, z_blk, gate_blk, tk):
    bsz, n, _ = zb3.shape
    groups, gd, _ = w_f.shape
    cn, sn, cc, sc = tables
    return pl.pallas_call(
        functools.partial(_fnet_kernel, groups=groups, norm=1.0 / math.sqrt(n * gd)),
        grid=(bsz, n // tk),
        in_specs=[
            pl.BlockSpec((1, n, wbr), lambda b, k: (b, 0, z_blk)),
            pl.BlockSpec((1, tk, wbr), lambda b, k: (b, k, gate_blk)),
            pl.BlockSpec((gd, gd), lambda b, k: (0, 0)),
            pl.BlockSpec((gd, gd), lambda b, k: (0, 0)),
            pl.BlockSpec((tk, n), lambda b, k: (k, 0)),
            pl.BlockSpec((tk, n), lambda b, k: (k, 0)),
            pl.BlockSpec((groups, gd, gd), lambda b, k: (0, 0, 0)),
            pl.BlockSpec((1, wbr), lambda b, k: (0, 0)),
        ],
        out_specs=pl.BlockSpec((1, tk, wbr), lambda b, k: (b, k, 0)),
        out_shape=jax.ShapeDtypeStruct((bsz, n, wbr), BF16),
        scratch_shapes=[pltpu.VMEM((n, wbr), BF16), pltpu.VMEM((n, wbr), BF16)],
        compiler_params=_cparams(("parallel", "arbitrary")),
        name="fnet_branch",
    )(zb3, zf3, cc, sc, cn, sn, w_f.astype(BF16), b_f.reshape(1, wbr))


def _nattn_plan(rows, width, win_h, win_w):
    kh, kw = min(win_h, rows), min(win_w, width)
    qb = Q_ROWS
    wr = min(kh + qb - 1, rows)
    nblk = rows // qb
    r0 = np.arange(nblk) * qb
    ks = np.clip(r0 - kh // 2, 0, rows - wr)
    r = r0[:, None] + np.arange(qb)[None]
    rs = np.clip(r - kh // 2, 0, rows - kh)
    kr = ks[:, None] + np.arange(wr)[None]
    row_ok = (kr[:, None, :] >= rs[:, :, None]) & (kr[:, None, :] < rs[:, :, None] + kh)
    assert row_ok.sum(-1).min() == kh
    dri = np.where(row_ok, kr[:, None, :] - r[:, :, None] + win_h - 1, 0)
    key = np.concatenate([dri.reshape(nblk, -1), row_ok.reshape(nblk, -1)], axis=1)
    _, first, inv = np.unique(key, axis=0, return_index=True, return_inverse=True)
    qc = np.arange(width)
    col_start = np.clip(qc - kw // 2, 0, width - kw)
    in_win = (qc[None, :] >= col_start[:, None]) & (qc[None, :] < col_start[:, None] + kw)
    dci = np.clip(qc[None, :] - qc[:, None], -(kw - 1), kw - 1) + win_w - 1
    return dict(qb=qb, wr=wr, nblk=nblk, ks=ks.astype(np.int32), pat=inv.reshape(-1).astype(np.int32),
                dri=dri[first], row_ok=row_ok[first], in_win=in_win, dci=dci)


def _nattn_bias(rpb, plan, width):
    dri, row_ok = plan["dri"], plan["row_ok"]
    npat, qb, wr = dri.shape
    heads, _, n_dc = rpb.shape
    rows = jnp.take(rpb.astype(F32), jnp.asarray(dri.reshape(-1)), axis=1)
    onehot = (plan["dci"].reshape(-1)[None, :] == np.arange(n_dc)[:, None]).astype(np.float32)
    vals = jnp.einsum("hrc,cw->hrw", rows, jnp.asarray(onehot), precision=HIGHEST)
    vals = vals.reshape(heads, npat, qb, wr, width, width).transpose(1, 0, 2, 4, 3, 5)
    valid = row_ok[:, None, :, None, :, None] & plan["in_win"][None, None, None, :, None, :]
    vals = jnp.where(jnp.asarray(valid), vals, NEG_INF)
    return vals.reshape(npat, heads, qb * width, wr * width)


def _softmax_pv(s_band, s_ctx, v_band, v_ctx):
    m = jnp.maximum(jnp.max(s_band, axis=-1, keepdims=True), jnp.max(s_ctx, axis=-1, keepdims=True))
    p_band = jnp.exp(s_band - m)
    p_ctx = jnp.exp(s_ctx - m)
    den = jnp.sum(p_band, axis=-1, keepdims=True) + jnp.sum(p_ctx, axis=-1, keepdims=True)
    out = _dot(p_band.astype(BF16), v_band) + _dot(p_ctx.astype(BF16), v_ctx)
    return out / den


def _nattn_kernel(ks_ref, pat_ref, q_ref, k_ref, v_ref, ck_ref, cv_ref, g_ref, bias_ref, o_ref,
                  *, heads, width, nkeys):
    blk = pl.program_id(1)
    start = pl.multiple_of(ks_ref[blk] * width, width)
    pat = pat_ref[blk]
    dh = q_ref.shape[2] // heads
    scale = dh ** -0.5
    gate = _silu(g_ref[0])
    for h in range(heads):
        cols = slice(h * dh, (h + 1) * dh)
        kb = k_ref[0, pl.ds(start, nkeys), cols]
        vb = v_ref[0, pl.ds(start, nkeys), cols]
        s_band = _dot_t(q_ref[0, :, cols], kb) * scale + bias_ref[pat, h]
        s_ctx = _dot_t(q_ref[0, :, cols], ck_ref[0, :, cols]) * scale
        out = _softmax_pv(s_band, s_ctx, vb, cv_ref[0, :, cols])
        o_ref[0, :, cols] = (out * gate[:, cols]).astype(BF16)


def _nattn(zxb, zcb, zxf, bias, plan, heads, wbr, blks, gate_blk):
    bsz, n, _ = zxb.shape
    lc = zcb.shape[1]
    k_blk, v_blk, q_blk = blks
    nq = plan["qb"] * GRID_W
    nkeys = plan["wr"] * GRID_W
    grid_spec = pltpu.PrefetchScalarGridSpec(
        num_scalar_prefetch=2,
        grid=(bsz, plan["nblk"]),
        in_specs=[
            pl.BlockSpec((1, nq, wbr), lambda b, i, *_: (b, i, q_blk)),
            pl.BlockSpec((1, n, wbr), lambda b, i, *_: (b, 0, k_blk)),
            pl.BlockSpec((1, n, wbr), lambda b, i, *_: (b, 0, v_blk)),
            pl.BlockSpec((1, lc, wbr), lambda b, i, *_: (b, 0, k_blk)),
            pl.BlockSpec((1, lc, wbr), lambda b, i, *_: (b, 0, v_blk)),
            pl.BlockSpec((1, nq, wbr), lambda b, i, *_: (b, i, gate_blk)),
            pl.BlockSpec(bias.shape, lambda b, i, *_: (0, 0, 0, 0)),
        ],
        out_specs=pl.BlockSpec((1, nq, wbr), lambda b, i, *_: (b, i, 0)),
    )
    return pl.pallas_call(
        functools.partial(_nattn_kernel, heads=heads, width=GRID_W, nkeys=nkeys),
        grid_spec=grid_spec,
        out_shape=jax.ShapeDtypeStruct((bsz, n, wbr), BF16),
        compiler_params=_cparams(("parallel", "arbitrary")),
        name="nattn_branch",
    )(jnp.asarray(plan["ks"]), jnp.asarray(plan["pat"]), zxb, zxb, zxb, zcb, zcb, zxf, bias)


def _cattn_kernel(q_ref, k_ref, v_ref, g_ref, o_ref, *, heads):
    dh = q_ref.shape[2] // heads
    scale = dh ** -0.5
    gate = _silu(g_ref[0])
    for h in range(heads):
        cols = slice(h * dh, (h + 1) * dh)
        s = _dot_t(q_ref[0, :, cols], k_ref[0, :, cols]) * scale
        p = jnp.exp(s - jnp.max(s, axis=-1, keepdims=True))
        out = _dot(p.astype(BF16), v_ref[0, :, cols]) / jnp.sum(p, axis=-1, keepdims=True)
        o_ref[0, :, cols] = (out * gate[:, cols]).astype(BF16)


def _cattn(zcb, zcf, heads, wbr, blks, gate_blk):
    bsz, lc, _ = zcb.shape
    k_blk, v_blk, q_blk = blks
    spec = lambda blk: pl.BlockSpec((1, lc, wbr), lambda b: (b, 0, blk))
    return pl.pallas_call(
        functools.partial(_cattn_kernel, heads=heads),
        grid=(bsz,),
        in_specs=[spec(q_blk), spec(k_blk), spec(v_blk), spec(gate_blk)],
        out_specs=spec(0),
        out_shape=jax.ShapeDtypeStruct((bsz, lc, wbr), BF16),
        compiler_params=_cparams(("parallel",)),
        name="cattn_branch",
    )(zcb, zcb, zcb, zcf)


def _s5_discretise(a_re, a_im, log_dt, b_re, b_im):
    dt = jnp.exp(log_dt)[..., None]
    lr, li = a_re * dt, a_im * dt
    mag = jnp.exp(lr)
    abar_re, abar_im = mag * jnp.cos(li), mag * jnp.sin(li)
    den = a_re * a_re + a_im * a_im
    f_re = ((abar_re - 1.0) * a_re + abar_im * a_im) / den
    f_im = (abar_im * a_re - (abar_re - 1.0) * a_im) / den
    bb_re = f_re[..., None] * b_re - f_im[..., None] * b_im
    bb_im = f_re[..., None] * b_im + f_im[..., None] * b_re
    return abar_re, abar_im, bb_re, bb_im


def _block_diag(m, gpb):
    two, g, a, b = m.shape
    nb = g // gpb
    m = m.reshape(two, nb, gpb, a, b)
    eye = jnp.eye(gpb, dtype=m.dtype)
    out = m[:, :, :, :, None, :] * eye[None, None, :, None, :, None]
    return out.reshape(two, nb, gpb * a, gpb * b)


def _s5_params(a_re, a_im, log_dt, b_re, b_im, c_re, c_im):
    f32 = lambda t: t.astype(F32)
    a_re, a_im, log_dt, b_re, b_im, c_re, c_im = map(f32, (a_re, a_im, log_dt, b_re, b_im, c_re, c_im))
    _, g, p, h = b_re.shape
    gpb = min(g, max(1, LANES // h))
    abar_re, abar_im, bb_re, bb_im = _s5_discretise(a_re, a_im, log_dt, b_re, b_im)
    tr = lambda t: jnp.swapaxes(t, -1, -2)
    bmat = jnp.concatenate([_block_diag(tr(bb_re), gpb), _block_diag(tr(bb_im), gpb)], axis=-1)
    cmat = jnp.stack([_block_diag(tr(c_re), gpb), _block_diag(-tr(c_im), gpb)], axis=2)
    abar = jnp.concatenate([abar_re.reshape(2, 1, g * p), abar_im.reshape(2, 1, g * p)], axis=-1)
    return dict(bmat=bmat.astype(BF16), cmat=cmat.astype(BF16), abar=abar)


def _interleave_perm(nseq, rows):
    src = np.arange(nseq * rows)
    dst = (src % rows) * nseq + src // rows
    pm = np.zeros((nseq * rows, nseq * rows), np.float32)
    pm[dst, src] = 1.0
    return pm


def _cmul_add(ar, ai, xr, xi, br, bi):
    return ar * xr - ai * xi + br, ar * xi + ai * xr + bi


def _permute_rows_exact(pm, y):
    hi = y.astype(BF16)
    r1 = y - hi.astype(F32)
    mid = r1.astype(BF16)
    lo = (r1 - mid.astype(F32)).astype(BF16)
    return _dot(pm, hi) + _dot(pm, mid) + _dot(pm, lo)


def _s5_kernel(*refs, reverse, final, tiles_per_blk):
    if final:
        (u_ref, h0_ref, pm_ref, bm_ref, cm_ref, a_ref, pmt_ref, yf_ref, g_ref, d_ref, wg_ref, bg_ref,
         o_ref, ht_ref, xs_ref, xb_ref, st_ref) = refs
    else:
        (u_ref, h0_ref, pm_ref, bm_ref, cm_ref, a_ref,
         o_ref, ht_ref, xs_ref, xb_ref, st_ref) = refs
    nseq, tcb, wbr = u_ref.shape
    n_tiles, lt = xs_ref.shape[0] // 2, xs_ref.shape[2]
    sp = n_tiles * lt
    nb = bm_ref.shape[1]
    bw_in, bw_st = wbr // nb, sp // nb
    tpb = bw_st // lt
    re_cols = lambda t: slice(t * lt, (t + 1) * lt)
    im_cols = lambda t: slice(sp + t * lt, sp + (t + 1) * lt)

    @pl.when(pl.program_id(1) == 0)
    def _():
        st_ref[...] = h0_ref[...]

    u = u_ref[...].reshape(nseq * tcb, wbr)
    ub = _dot(pm_ref[...], u.astype(BF16)).astype(BF16)
    for n in range(nb):
        bu = _dot(ub[:, n * bw_in:(n + 1) * bw_in], bm_ref[0, n])
        for t in range(tpb):
            xs_ref[n * tpb + t] = bu[:, t * lt:(t + 1) * lt]
            xs_ref[n_tiles + n * tpb + t] = bu[:, bw_st + t * lt:bw_st + (t + 1) * lt]

    npairs = tcb // 2
    for q in range(n_tiles // tiles_per_blk):
        tiles = range(q * tiles_per_blk, (q + 1) * tiles_per_blk)
        ar = [jnp.broadcast_to(a_ref[0, :, re_cols(t)], (nseq, lt)) for t in tiles]
        ai = [jnp.broadcast_to(a_ref[0, :, im_cols(t)], (nseq, lt)) for t in tiles]

        def step(i, st):
            jj = npairs - 1 - i if reverse else i
            r0 = pl.multiple_of(jj * 2 * nseq, 2 * nseq)
            out = []
            for idx, t in enumerate(tiles):
                dr, di = xs_ref[t, pl.ds(r0, 2 * nseq), :], xs_ref[n_tiles + t, pl.ds(r0, 2 * nseq), :]
                first, second = (slice(nseq, None), slice(0, nseq)) if reverse else (slice(0, nseq), slice(nseq, None))
                x0 = _cmul_add(ar[idx], ai[idx], st[2 * idx], st[2 * idx + 1], dr[first], di[first])
                x1 = _cmul_add(ar[idx], ai[idx], x0[0], x0[1], dr[second], di[second])
                lo, hi = (x1, x0) if reverse else (x0, x1)
                xb_ref[pl.ds(r0, 2 * nseq), re_cols(t)] = jnp.concatenate([lo[0], hi[0]], axis=0).astype(BF16)
                xb_ref[pl.ds(r0, 2 * nseq), im_cols(t)] = jnp.concatenate([lo[1], hi[1]], axis=0).astype(BF16)
                out += [x1[0], x1[1]]
            return tuple(out)

        init = []
        for t in tiles:
            init += [st_ref[:, re_cols(t)], st_ref[:, im_cols(t)]]
        last = lax.fori_loop(0, npairs, step, tuple(init), unroll=4)
        for idx, t in enumerate(tiles):
            st_ref[:, re_cols(t)] = last[2 * idx]
            st_ref[:, im_cols(t)] = last[2 * idx + 1]

    ht_ref[...] = st_ref[...]

    parts = []
    for n in range(nb):
        parts.append(_dot(xb_ref[:, n * bw_st:(n + 1) * bw_st], cm_ref[0, n, 0])
                     + _dot(xb_ref[:, sp + n * bw_st:sp + (n + 1) * bw_st], cm_ref[0, n, 1]))
    y = parts[0] if nb == 1 else jnp.concatenate(parts, axis=-1)

    if final:
        y = _permute_rows_exact(pmt_ref[...], yf_ref[...].reshape(nseq * tcb, wbr) + y)
        tot = jax.nn.gelu(y + d_ref[...] * u)
        glu = jax.nn.sigmoid(_dot(tot.astype(BF16), wg_ref[...]) + bg_ref[...])
        gate = _silu(g_ref[...].reshape(nseq * tcb, wbr))
        o_ref[...] = (tot * glu * gate).astype(BF16).reshape(nseq, tcb, wbr)
    else:
        o_ref[...] = y.reshape(nseq, tcb, wbr)


def _s5_pass(z3, h0, prm, dirn, wbr, u_blk, tcb, final_args=None):
    bsz, n, _ = z3.shape
    assert bsz % SUBLANES == 0 and n % tcb == 0 and tcb % 2 == 0
    nchunks = n // tcb
    reverse = dirn == 1
    final = final_args is not None
    sp2 = prm["abar"].shape[-1]
    bmat, cmat = prm["bmat"], prm["cmat"]
    sp = sp2 // 2
    lt = min(LANES, sp // bmat.shape[1])
    tiles_per_blk = min(sp // lt, 4)
    rows = SUBLANES * tcb
    pm = _interleave_perm(SUBLANES, tcb)
    chunk = (lambda k: nchunks - 1 - k) if reverse else (lambda k: k)
    seq_spec = lambda blk: pl.BlockSpec((SUBLANES, tcb, wbr), lambda g, k: (g, chunk(k), blk))
    const = lambda shape: pl.BlockSpec(shape, lambda g, k: (0,) * len(shape))
    dir_spec = lambda shape: pl.BlockSpec((1,) + shape[1:], lambda g, k: (dirn,) + (0,) * (len(shape) - 1))
    in_specs = [
        seq_spec(u_blk),
        pl.BlockSpec((SUBLANES, sp2), lambda g, k: (g, 0)),
        const((rows, rows)),
        dir_spec(bmat.shape), dir_spec(cmat.shape), dir_spec(prm["abar"].shape),
    ]
    args = [z3, h0, jnp.asarray(pm, BF16), bmat, cmat, prm["abar"]]
    if final:
        yf, gate_blk, d, w_glu, b_glu = final_args
        in_specs += [const((rows, rows)), seq_spec(0), seq_spec(gate_blk),
                     const((1, wbr)), const((wbr, wbr)), const((1, wbr))]
        args += [jnp.asarray(pm.T, BF16), yf, z3, d.reshape(1, wbr).astype(F32), w_glu.astype(BF16),
                 b_glu.reshape(1, wbr).astype(F32)]
    return pl.pallas_call(
        functools.partial(_s5_kernel, reverse=reverse, final=final, tiles_per_blk=tiles_per_blk),
        grid=(bsz // SUBLANES, nchunks),
        in_specs=in_specs,
        out_specs=[seq_spec(0), pl.BlockSpec((SUBLANES, sp2), lambda g, k: (g, 0))],
        out_shape=[
            jax.ShapeDtypeStruct((bsz, n, wbr), BF16 if final else F32),
            jax.ShapeDtypeStruct((bsz, sp2), F32),
        ],
        scratch_shapes=[
            pltpu.VMEM((sp2 // lt, rows, lt), F32),
            pltpu.VMEM((rows, sp2), BF16),
            pltpu.VMEM((SUBLANES, sp2), F32),
        ],
        compiler_params=_cparams(("parallel", "arbitrary")),
        name="s5_final" if final else "s5_fwd",
    )(*args)


def _s5(z3, h0, prm, d, w_glu, b_glu, wbr, u_blk, gate_blk, tcb):
    yf, hf = _s5_pass(z3, h0[0], prm, 0, wbr, u_blk, tcb)
    out, hb = _s5_pass(z3, h0[1], prm, 1, wbr, u_blk, tcb, final_args=(yf, gate_blk, d, w_glu, b_glu))
    return out, jnp.stack([hf, hb])


def _merge_kernel(h_ref, y0, y1, y2, y3, w0, w1, w2, w3, bm_ref, u0, u1, u2, u3, o_ref):
    h = h_ref[...]
    acc = None
    for i, (y, w, u) in enumerate(zip((y0, y1, y2, y3), (w0, w1, w2, w3), (u0, u1, u2, u3))):
        term = jax.nn.sigmoid(_dot(h, w[...]) + bm_ref[i]) * _dot(y[...], u[0])
        acc = term if acc is None else acc + term
    o_ref[...] = acc.astype(BF16)


def _merge(h2, ys, w_m, b_m, w_up, tm, tn):
    t, d = h2.shape
    wbr = ys[0].shape[1]
    ns = d // tn
    y_spec = pl.BlockSpec((tm, wbr), lambda i, s: (i, 0))
    w_specs = [pl.BlockSpec((d, tn), functools.partial(lambda i, s, br: (0, br * ns + s), br=br))
               for br in range(N_BRANCH)]
    u_specs = [pl.BlockSpec((1, wbr, tn), functools.partial(lambda i, s, br: (br, 0, s), br=br))
               for br in range(N_BRANCH)]
    return pl.pallas_call(
        _merge_kernel,
        grid=(t // tm, ns),
        in_specs=[
            pl.BlockSpec((tm, d), lambda i, s: (i, 0)),
            y_spec, y_spec, y_spec, y_spec,
            *w_specs,
            pl.BlockSpec((N_BRANCH, 1, tn), lambda i, s: (0, 0, s)),
            *u_specs,
        ],
        out_specs=pl.BlockSpec((tm, tn), lambda i, s: (i, s)),
        out_shape=jax.ShapeDtypeStruct((t, d), BF16),
        compiler_params=_cparams(("parallel", "arbitrary")),
        name="merge",
    )(h2, *ys, w_m, w_m, w_m, w_m, b_m, w_up, w_up, w_up, w_up)


def _outproj_kernel(*refs, d, alpha, sub, emit_h):
    if emit_h:
        m_ref, x_ref, mod_ref, modn_ref, w_ref, g_ref, b_ref, o_ref, h_ref = refs
    else:
        m_ref, x_ref, mod_ref, w_ref, g_ref, b_ref, o_ref = refs
    gate = mod_ref[0, :, 2 * d:3 * d]
    for r in range(m_ref.shape[0] // sub):
        rows = slice(r * sub, (r + 1) * sub)
        out = _dot(m_ref[rows, :], w_ref[...])
        xn = _ln(alpha * x_ref[rows, :] + gate * out) * g_ref[...] + b_ref[...]
        o_ref[rows, :] = xn
        if emit_h:
            h_ref[rows, :] = _modulated(xn, modn_ref, d).astype(BF16)


def _out_proj(m2, x2, mod, mod_next, w_o, ln_g, ln_b, alpha, tm):
    t, d = x2.shape
    tiles_per_mod = (t // mod.shape[0]) // tm
    emit_h = mod_next is not None
    row = pl.BlockSpec((tm, d), lambda i: (i, 0))
    vec = pl.BlockSpec((1, d), lambda i: (0, 0))
    mods = [_mod_spec(d, tiles_per_mod)] * (2 if emit_h else 1)
    mod_args = (mod, mod_next) if emit_h else (mod,)
    out = pl.pallas_call(
        functools.partial(_outproj_kernel, d=d, alpha=alpha, sub=_pick_tile(tm, 128), emit_h=emit_h),
        grid=(t // tm,),
        in_specs=[row, row, *mods, pl.BlockSpec((d, d), lambda i: (0, 0)), vec, vec],
        out_specs=[row, row] if emit_h else row,
        out_shape=([jax.ShapeDtypeStruct((t, d), F32), jax.ShapeDtypeStruct((t, d), BF16)] if emit_h
                   else jax.ShapeDtypeStruct((t, d), F32)),
        compiler_params=_cparams(("parallel",)),
        name="out_proj",
    )(m2, x2, *mod_args, w_o, ln_g.reshape(1, d), ln_b.reshape(1, d))
    return out if emit_h else (out, None)


def kernel(x, c, ctx, c_ctx, w_ada, b_ada, w_in, b_in, w_sgu, b_sgu, w_fnet, b_fnet, rpb, s5_a_re, s5_a_im,
           s5_log_dt, s5_b_re, s5_b_im, s5_c_re, s5_c_im, s5_d, s5_w_glu, s5_b_glu, w_up, w_o, ln_g, ln_b):
    bsz, n, d = x.shape
    lc = ctx.shape[1]
    depth = w_ada.shape[0]
    wbr = d // N_BRANCH
    heads = rpb.shape[1]
    win_h, win_w = (rpb.shape[2] + 1) // 2, (rpb.shape[3] + 1) // 2
    rows = n // GRID_W
    alpha = (2 * depth) ** 0.25
    sp2 = 2 * s5_a_re.shape[2] * s5_a_re.shape[3]

    W = wbr
    take = lambda a, spans: jnp.concatenate([a[..., lo * W:hi * W] for lo, hi in spans], axis=-1)
    spans16 = ((1, 4), (6, 7))
    spans32 = ((4, 6), (0, 1), (7, 11))
    off_merge = 11 * W
    blk_k, blk_v, blk_q, blk_b = 0, 1, 2, 3
    blk_uv, blk_u, blk_gate = 0, 2, 3
    n16, n32 = 4 * W, 7 * W

    n_rows = -(-(bsz + 1) // 8) * 8
    cond = jnp.zeros((n_rows, d), F32).at[:bsz].set(c).at[bsz].set(c_ctx)
    ada = _ada_all(cond, w_ada, b_ada)
    mod_x = [ada[l, :bsz].reshape(bsz, 1, 3 * d) for l in range(depth)]
    mod_c = [ada[l, bsz:bsz + 1].reshape(1, 1, 3 * d) for l in range(depth)]

    w16 = take(w_in, spans16).astype(BF16)
    w32 = take(w_in, spans32).astype(BF16)
    w_m = w_in[:, :, off_merge:].astype(BF16)
    w_up_b = w_up.astype(BF16)
    w_o_b = w_o.astype(BF16)

    plan = _nattn_plan(rows, GRID_W, win_h, win_w)
    gd_f = w_fnet.shape[2]
    cc, sc = _dft_tables(gd_f)
    dft_x = _dft_tables(n) + (cc, sc)
    dft_c = _dft_tables(lc) + (cc, sc)

    tcb_x, tcb_c = _pick_tile(n, S5_ROWS), _pick_tile(lc, S5_ROWS)
    tx, tc_all = bsz * n, bsz * lc
    tm_x, tm_c = _pick_tile(n, TM_MATMUL), _pick_tile(tc_all, TM_MATMUL)
    tm_o_x, tm_o_c = _pick_tile(n, TM_OUT), _pick_tile(tc_all, TM_OUT)
    tn32 = n32 // 2 if (n32 // 2) % LANES == 0 else W

    xt = x.reshape(tx, d)
    ct = ctx.reshape(tc_all, d)
    for l in range(depth):
        need_ctx = l < depth - 1
        b16 = take(b_in[l], spans16).reshape(1, n16)
        b32 = take(b_in[l], spans32).reshape(1, n32)
        b_m = b_in[l, off_merge:].reshape(N_BRANCH, 1, d)

        zxb, hx = _mod_proj(xt, mod_x[l], w16[l], b16, n16, tm_o_x, BF16)
        zxb = zxb.reshape(bsz, n, n16)
        zxf = _in_proj(hx, w32[l], b32, n32, tm_x, tn32, F32).reshape(bsz, n, n32)
        c16, c32 = (n16, n32) if need_ctx else (2 * W, 3 * W)
        zcb, hc = _mod_proj(ct, mod_c[l], w16[l], b16, c16, tm_o_c, BF16)
        zcb = zcb.reshape(bsz, lc, c16)
        zcf = _in_proj(hc, w32[l], b32, c32, tm_c, tn32 if need_ctx else W, F32).reshape(bsz, lc, c32)

        prm = _s5_params(s5_a_re[l], s5_a_im[l], s5_log_dt[l], s5_b_re[l], s5_b_im[l], s5_c_re[l], s5_c_im[l])
        gate_c = blk_gate + 3 if need_ctx else blk_u
        yd_c, h_ctx = _s5(zcf, jnp.zeros((2, bsz, sp2), F32), prm, s5_d[l], s5_w_glu[l], s5_b_glu[l],
                          W, blk_u, gate_c, tcb_c)
        yd_x, _ = _s5(zxf, h_ctx, prm, s5_d[l], s5_w_glu[l], s5_b_glu[l], W, blk_u, blk_gate + 3, tcb_x)

        bias = _nattn_bias(rpb[l], plan, GRID_W)
        yc_x = _nattn(zxb, zcb, zxf, bias, plan, heads, W, (blk_k, blk_v, blk_q), blk_gate + 2)
        ya_x = _sgu(zxf, w_sgu[l], b_sgu[l], W, blk_uv, blk_gate + 0, _pick_tile(n, T_SGU))
        yb_x = _fnet(zxb, zxf, w_fnet[l], b_fnet[l], dft_x, W, blk_b, blk_gate + 1, _pick_tile(n, T_FNET))

        ys = [y.reshape(tx, W) for y in (ya_x, yb_x, yc_x, yd_x)]
        m = _merge(hx, ys, w_m[l], b_m, w_up_b[l], tm_x, _pick_tile(d, TN_MERGE))
        xt_new, _ = _out_proj(m, xt, mod_x[l], None, w_o_b[l], ln_g[l], ln_b[l], alpha, tm_o_x)

        if need_ctx:
            yc_c = _cattn(zcb, zcf, heads, W, (blk_k, blk_v, blk_q), blk_gate + 2)
            ya_c = _sgu(zcf, w_sgu[l], b_sgu[l], W, blk_uv, blk_gate + 0, _pick_tile(lc, T_SGU))
            yb_c = _fnet(zcb, zcf, w_fnet[l], b_fnet[l], dft_c, W, blk_b, blk_gate + 1, _pick_tile(lc, T_FNET))
            ys_c = [y.reshape(tc_all, W) for y in (ya_c, yb_c, yc_c, yd_c)]
            m_c = _merge(hc, ys_c, w_m[l], b_m, w_up_b[l], tm_c, _pick_tile(d, TN_MERGE))
            ct, _ = _out_proj(m_c, ct, mod_c[l], None, w_o_b[l], ln_g[l], ln_b[l], alpha, tm_o_c)
        xt = xt_new
    return xt.reshape(bsz, n, d)
```

```python
import functools
import math

import jax
import jax.numpy as jnp
import numpy as np
from jax import lax
from jax.experimental import pallas as pl
from jax.experimental.pallas import tpu as pltpu

GRID_W = 64
N_BRANCH = 4
LN_EPS = 1e-6
NEG_INF = -1e30
Q_ROWS = 4
SUBLANES = 8
LANES = 128
S5_ROWS = 64
TM_MATMUL = 1024
TM_OUT = 512
T_SGU = 512
T_FNET = 1024
TN_MERGE = 512
VMEM_LIMIT = 56 * 1024 * 1024

F32 = jnp.float32
BF16 = jnp.bfloat16
HIGHEST = lax.Precision.HIGHEST


def _cparams(sem):
    return pltpu.CompilerParams(dimension_semantics=sem, vmem_limit_bytes=VMEM_LIMIT)


def _dot(a, b):
    return jnp.dot(a, b, preferred_element_type=F32)


def _dot_t(a, b):
    return lax.dot_general(a, b, (((1,), (1,)), ((), ())), preferred_element_type=F32)


def _ln(x):
    mu = jnp.mean(x, axis=-1, keepdims=True)
    xc = x - mu
    var = jnp.mean(xc * xc, axis=-1, keepdims=True)
    return xc * lax.rsqrt(var + LN_EPS)


def _silu(x):
    return x * jax.nn.sigmoid(x)


def _pick_tile(n, target):
    t = min(n, target)
    while n % t:
        t //= 2
    return t


def _mod_spec(d, tiles_per_mod):
    return pl.BlockSpec((1, 1, 3 * d), lambda i, *_: (i // tiles_per_mod, 0, 0))


def _ada_kernel(s_ref, w_ref, b_ref, o_ref):
    s = _silu(s_ref[...]).astype(BF16)
    o_ref[0] = _dot(s, w_ref[0].astype(BF16)) + b_ref[0]


def _ada_all(cond, w_ada, b_ada):
    depth, d, d3 = w_ada.shape
    r = cond.shape[0]
    tn = _pick_tile(d3, 1024)
    return pl.pallas_call(
        _ada_kernel,
        grid=(depth, d3 // tn),
        in_specs=[
            pl.BlockSpec((r, d), lambda l, j: (0, 0)),
            pl.BlockSpec((1, d, tn), lambda l, j: (l, 0, j)),
            pl.BlockSpec((1, 1, tn), lambda l, j: (l, 0, j)),
        ],
        out_specs=pl.BlockSpec((1, r, tn), lambda l, j: (l, 0, j)),
        out_shape=jax.ShapeDtypeStruct((depth, r, d3), F32),
        compiler_params=_cparams(("arbitrary", "arbitrary")),
        name="ada_proj",
    )(cond, w_ada, b_ada.reshape(depth, 1, d3))


def _modulated(x, mod_ref, d):
    return _ln(x) * (1.0 + mod_ref[0, :, d:2 * d]) + mod_ref[0, :, 0:d]


def _modproj_kernel(x_ref, mod_ref, w_ref, b_ref, o_ref, h_ref, *, d, sub):
    for r in range(x_ref.shape[0] // sub):
        rows = slice(r * sub, (r + 1) * sub)
        h = _modulated(x_ref[rows, :], mod_ref, d).astype(BF16)
        h_ref[rows, :] = h
        o_ref[rows, :] = (_dot(h, w_ref[...]) + b_ref[...]).astype(o_ref.dtype)


def _mod_proj(x2, mod, w, b, ncols, tm, dtype):
    t, d = x2.shape
    tiles_per_mod = (t // mod.shape[0]) // tm
    return pl.pallas_call(
        functools.partial(_modproj_kernel, d=d, sub=_pick_tile(tm, 128)),
        grid=(t // tm,),
        in_specs=[
            pl.BlockSpec((tm, d), lambda i: (i, 0)),
            _mod_spec(d, tiles_per_mod),
            pl.BlockSpec((d, ncols), lambda i: (0, 0)),
            pl.BlockSpec((1, ncols), lambda i: (0, 0)),
        ],
        out_specs=[pl.BlockSpec((tm, ncols), lambda i: (i, 0)), pl.BlockSpec((tm, d), lambda i: (i, 0))],
        out_shape=[jax.ShapeDtypeStruct((t, ncols), dtype), jax.ShapeDtypeStruct((t, d), BF16)],
        compiler_params=_cparams(("parallel",)),
        name="mod_proj",
    )(x2, mod, w, b)


def _inproj_kernel(h_ref, w_ref, b_ref, o_ref):
    o_ref[...] = (_dot(h_ref[...], w_ref[...]) + b_ref[...]).astype(o_ref.dtype)


def _in_proj(h2, w, b, ncols, tm, tn, dtype):
    t, d = h2.shape
    return pl.pallas_call(
        _inproj_kernel,
        grid=(t // tm, ncols // tn),
        in_specs=[
            pl.BlockSpec((tm, d), lambda i, j: (i, 0)),
            pl.BlockSpec((d, tn), lambda i, j: (0, j)),
            pl.BlockSpec((1, tn), lambda i, j: (0, j)),
        ],
        out_specs=pl.BlockSpec((tm, tn), lambda i, j: (i, j)),
        out_shape=jax.ShapeDtypeStruct((t, ncols), dtype),
        compiler_params=_cparams(("parallel", "arbitrary")),
        name="in_proj",
    )(h2, w, b)


def _sgu_kernel(uv_ref, g_ref, w_ref, b_ref, o_ref, *, wbr, chunk, groups):
    uv = jax.nn.gelu(uv_ref[0])
    u = uv[:, :wbr]
    v = _ln(uv[:, wbr:]).astype(BF16)
    gate = _silu(g_ref[0])
    gd = wbr // groups
    for k in range(uv.shape[0] // chunk):
        rows = slice(k * chunk, (k + 1) * chunk)
        for g in range(groups):
            cols = slice(g * gd, (g + 1) * gd)
            mixed = _dot(w_ref[g], v[rows, cols]) + b_ref[g]
            o_ref[0, rows, cols] = (u[rows, cols] * mixed * gate[rows, cols]).astype(BF16)


def _sgu(z3, w_sgu, b_sgu, wbr, uv_blk, gate_blk, ta):
    bsz, n, _ = z3.shape
    groups, chunk, _ = w_sgu.shape
    return pl.pallas_call(
        functools.partial(_sgu_kernel, wbr=wbr, chunk=chunk, groups=groups),
        grid=(bsz, n // ta),
        in_specs=[
            pl.BlockSpec((1, ta, 2 * wbr), lambda b, i: (b, i, uv_blk)),
            pl.BlockSpec((1, ta, wbr), lambda b, i: (b, i, gate_blk)),
            pl.BlockSpec((groups, chunk, chunk), lambda b, i: (0, 0, 0)),
            pl.BlockSpec((groups, chunk, 1), lambda b, i: (0, 0, 0)),
        ],
        out_specs=pl.BlockSpec((1, ta, wbr), lambda b, i: (b, i, 0)),
        out_shape=jax.ShapeDtypeStruct((bsz, n, wbr), BF16),
        compiler_params=_cparams(("parallel", "arbitrary")),
        name="sgu_branch",
    )(z3, z3, w_sgu.astype(BF16), b_sgu.reshape(groups, chunk, 1))


def _dft_tables(n):
    k = jnp.arange(n, dtype=jnp.int32)
    r = (k[:, None] * k[None, :]) % n
    ang = r.astype(F32) * (2.0 * math.pi / n)
    return jnp.cos(ang).astype(BF16), jnp.sin(ang).astype(BF16)


def _fnet_kernel(z_ref, g_ref, cc_ref, sc_ref, cn_ref, sn_ref, wf_ref, bf_ref, o_ref, p_ref, q_ref,
                 *, groups, norm):
    wbr = z_ref.shape[2]
    gd = wbr // groups

    @pl.when(pl.program_id(1) == 0)
    def _():
        for g in range(groups):
            cols = slice(g * gd, (g + 1) * gd)
            zg = z_ref[0, :, cols]
            p_ref[:, cols] = _dot(zg, cc_ref[...]).astype(BF16)
            q_ref[:, cols] = _dot(zg, sc_ref[...]).astype(BF16)

    f = (_dot(cn_ref[...], p_ref[...]) - _dot(sn_ref[...], q_ref[...])) * norm
    gate = _silu(g_ref[0])
    for g in range(groups):
        cols = slice(g * gd, (g + 1) * gd)
        y = _dot(f[:, cols].astype(BF16), wf_ref[g]) + bf_ref[:, cols]
        o_ref[0, :, cols] = (y * gate[:, cols]).astype(BF16)


def _fnet(zb3, zf3, w_f, b_f, tables, wbr, z_blk, gate_blk, tk):
    bsz, n, _ = zb3.shape
    groups, gd, _ = w_f.shape
    cn, sn, cc, sc = tables
    return pl.pallas_call(
        functools.partial(_fnet_kernel, groups=groups, norm=1.0 / math.sqrt(n * gd)),
        grid=(bsz, n // tk),
        in_specs=[
            pl.BlockSpec((1, n, wbr), lambda b, k: (b, 0, z_blk)),
            pl.BlockSpec((1, tk, wbr), lambda b, k: (b, k, gate_blk)),
            pl.BlockSpec((gd, gd), lambda b, k: (0, 0)),
            pl.BlockSpec((gd, gd), lambda b, k: (0, 0)),
            pl.BlockSpec((tk, n), lambda b, k: (k, 0)),
            pl.BlockSpec((tk, n), lambda b, k: (k, 0)),
            pl.BlockSpec((groups, gd, gd), lambda b, k: (0, 0, 0)),
            pl.BlockSpec((1, wbr), lambda b, k: (0, 0)),
        ],
        out_specs=pl.BlockSpec((1, tk, wbr), lambda b, k: (b, k, 0)),
        out_shape=jax.ShapeDtypeStruct((bsz, n, wbr), BF16),
        scratch_shapes=[pltpu.VMEM((n, wbr), BF16), pltpu.VMEM((n, wbr), BF16)],
        compiler_params=_cparams(("parallel", "arbitrary")),
        name="fnet_branch",
    )(zb3, zf3, cc, sc, cn, sn, w_f.astype(BF16), b_f.reshape(1, wbr))


def _nattn_plan(rows, width, win_h, win_w):
    kh, kw = min(win_h, rows), min(win_w, width)
    qb = Q_ROWS
    wr = min(kh + qb - 1, rows)
    nblk = rows // qb
    r0 = np.arange(nblk) * qb
    ks = np.clip(r0 - kh // 2, 0, rows - wr)
    r = r0[:, None] + np.arange(qb)[None]
    rs = np.clip(r - kh // 2, 0, rows - kh)
    kr = ks[:, None] + np.arange(wr)[None]
    row_ok = (kr[:, None, :] >= rs[:, :, None]) & (kr[:, None, :] < rs[:, :, None] + kh)
    assert row_ok.sum(-1).min() == kh
    dri = np.where(row_ok, kr[:, None, :] - r[:, :, None] + win_h - 1, 0)
    key = np.concatenate([dri.reshape(nblk, -1), row_ok.reshape(nblk, -1)], axis=1)
    _, first, inv = np.unique(key, axis=0, return_index=True, return_inverse=True)
    qc = np.arange(width)
    col_start = np.clip(qc - kw // 2, 0, width - kw)
    in_win = (qc[None, :] >= col_start[:, None]) & (qc[None, :] < col_start[:, None] + kw)
    dci = np.clip(qc[None, :] - qc[:, None], -(kw - 1), kw - 1) + win_w - 1
    return dict(qb=qb, wr=wr, nblk=nblk, ks=ks.astype(np.int32), pat=inv.reshape(-1).astype(np.int32),
                dri=dri[first], row_ok=row_ok[first], in_win=in_win, dci=dci)


def _nattn_bias(rpb, plan, width):
    dri, row_ok = plan["dri"], plan["row_ok"]
    npat, qb, wr = dri.shape
    heads, _, n_dc = rpb.shape
    rows = jnp.take(rpb.astype(F32), jnp.asarray(dri.reshape(-1)), axis=1)
    onehot = (plan["dci"].reshape(-1)[None, :] == np.arange(n_dc)[:, None]).astype(np.float32)
    vals = jnp.einsum("hrc,cw->hrw", rows, jnp.asarray(onehot), precision=HIGHEST)
    vals = vals.reshape(heads, npat, qb, wr, width, width).transpose(1, 0, 2, 4, 3, 5)
    valid = row_ok[:, None, :, None, :, None] & plan["in_win"][None, None, None, :, None, :]
    vals = jnp.where(jnp.asarray(valid), vals, NEG_INF)
    return vals.reshape(npat, heads, qb * width, wr * width)


def _softmax_pv(s_band, s_ctx, v_band, v_ctx):
    m = jnp.maximum(jnp.max(s_band, axis=-1, keepdims=True), jnp.max(s_ctx, axis=-1, keepdims=True))
    p_band = jnp.exp(s_band - m)
    p_ctx = jnp.exp(s_ctx - m)
    den = jnp.sum(p_band, axis=-1, keepdims=True) + jnp.sum(p_ctx, axis=-1, keepdims=True)
    out = _dot(p_band.astype(BF16), v_band) + _dot(p_ctx.astype(BF16), v_ctx)
    return out / den


def _nattn_kernel(ks_ref, pat_ref, q_ref, k_ref, v_ref, ck_ref, cv_ref, g_ref, bias_ref, o_ref,
                  *, heads, width, nkeys):
    blk = pl.program_id(1)
    start = pl.multiple_of(ks_ref[blk] * width, width)
    pat = pat_ref[blk]
    dh = q_ref.shape[2] // heads
    scale = dh ** -0.5
    gate = _silu(g_ref[0])
    head_cols = lambda h: slice(h * dh, (h + 1) * dh)

    def scores(h):
        cols = head_cols(h)
        kb = k_ref[0, pl.ds(start, nkeys), cols]
        s_band = _dot_t(q_ref[0, :, cols], kb) * scale + bias_ref[pat, h]
        s_ctx = _dot_t(q_ref[0, :, cols], ck_ref[0, :, cols]) * scale
        return s_band, s_ctx

    s = scores(0)
    for h in range(heads):
        nxt = scores(h + 1) if h + 1 < heads else None
        cols = head_cols(h)
        vb = v_ref[0, pl.ds(start, nkeys), cols]
        out = _softmax_pv(s[0], s[1], vb, cv_ref[0, :, cols])
        o_ref[0, :, cols] = (out * gate[:, cols]).astype(BF16)
        s = nxt


def _nattn(zxb, zcb, zxf, bias, plan, heads, wbr, blks, gate_blk):
    bsz, n, _ = zxb.shape
    lc = zcb.shape[1]
    k_blk, v_blk, q_blk = blks
    nq = plan["qb"] * GRID_W
    nkeys = plan["wr"] * GRID_W
    grid_spec = pltpu.PrefetchScalarGridSpec(
        num_scalar_prefetch=2,
        grid=(bsz, plan["nblk"]),
        in_specs=[
            pl.BlockSpec((1, nq, wbr), lambda b, i, *_: (b, i, q_blk)),
            pl.BlockSpec((1, n, wbr), lambda b, i, *_: (b, 0, k_blk)),
            pl.BlockSpec((1, n, wbr), lambda b, i, *_: (b, 0, v_blk)),
            pl.BlockSpec((1, lc, wbr), lambda b, i, *_: (b, 0, k_blk)),
            pl.BlockSpec((1, lc, wbr), lambda b, i, *_: (b, 0, v_blk)),
            pl.BlockSpec((1, nq, wbr), lambda b, i, *_: (b, i, gate_blk)),
            pl.BlockSpec(bias.shape, lambda b, i, *_: (0, 0, 0, 0)),
        ],
        out_specs=pl.BlockSpec((1, nq, wbr), lambda b, i, *_: (b, i, 0)),
    )
    return pl.pallas_call(
        functools.partial(_nattn_kernel, heads=heads, width=GRID_W, nkeys=nkeys),
        grid_spec=grid_spec,
        out_shape=jax.ShapeDtypeStruct((bsz, n, wbr), BF16),
        compiler_params=_cparams(("parallel", "arbitrary")),
        name="nattn_branch",
    )(jnp.asarray(plan["ks"]), jnp.asarray(plan["pat"]), zxb, zxb, zxb, zcb, zcb, zxf, bias)


def _cattn_kernel(q_ref, k_ref, v_ref, g_ref, o_ref, *, heads):
    dh = q_ref.shape[2] // heads
    scale = dh ** -0.5
    gate = _silu(g_ref[0])
    for h in range(heads):
        cols = slice(h * dh, (h + 1) * dh)
        s = _dot_t(q_ref[0, :, cols], k_ref[0, :, cols]) * scale
        p = jnp.exp(s - jnp.max(s, axis=-1, keepdims=True))
        out = _dot(p.astype(BF16), v_ref[0, :, cols]) / jnp.sum(p, axis=-1, keepdims=True)
        o_ref[0, :, cols] = (out * gate[:, cols]).astype(BF16)


def _cattn(zcb, zcf, heads, wbr, blks, gate_blk):
    bsz, lc, _ = zcb.shape
    k_blk, v_blk, q_blk = blks
    spec = lambda blk: pl.BlockSpec((1, lc, wbr), lambda b: (b, 0, blk))
    return pl.pallas_call(
        functools.partial(_cattn_kernel, heads=heads),
        grid=(bsz,),
        in_specs=[spec(q_blk), spec(k_blk), spec(v_blk), spec(gate_blk)],
        out_specs=spec(0),
        out_shape=jax.ShapeDtypeStruct((bsz, lc, wbr), BF16),
        compiler_params=_cparams(("parallel",)),
        name="cattn_branch",
    )(zcb, zcb, zcb, zcf)


def _s5_discretise(a_re, a_im, log_dt, b_re, b_im):
    dt = jnp.exp(log_dt)[..., None]
    lr, li = a_re * dt, a_im * dt
    mag = jnp.exp(lr)
    abar_re, abar_im = mag * jnp.cos(li), mag * jnp.sin(li)
    den = a_re * a_re + a_im * a_im
    f_re = ((abar_re - 1.0) * a_re + abar_im * a_im) / den
    f_im = (abar_im * a_re - (abar_re - 1.0) * a_im) / den
    bb_re = f_re[..., None] * b_re - f_im[..., None] * b_im
    bb_im = f_re[..., None] * b_im + f_im[..., None] * b_re
    return abar_re, abar_im, bb_re, bb_im


def _block_diag(m, gpb):
    two, g, a, b = m.shape
    nb = g // gpb
    m = m.reshape(two, nb, gpb, a, b)
    eye = jnp.eye(gpb, dtype=m.dtype)
    out = m[:, :, :, :, None, :] * eye[None, None, :, None, :, None]
    return out.reshape(two, nb, gpb * a, gpb * b)


def _s5_params(a_re, a_im, log_dt, b_re, b_im, c_re, c_im):
    f32 = lambda t: t.astype(F32)
    a_re, a_im, log_dt, b_re, b_im, c_re, c_im = map(f32, (a_re, a_im, log_dt, b_re, b_im, c_re, c_im))
    _, g, p, h = b_re.shape
    gpb = min(g, max(1, LANES // h))
    abar_re, abar_im, bb_re, bb_im = _s5_discretise(a_re, a_im, log_dt, b_re, b_im)
    tr = lambda t: jnp.swapaxes(t, -1, -2)
    bmat = jnp.concatenate([_block_diag(tr(bb_re), gpb), _block_diag(tr(bb_im), gpb)], axis=-1)
    cmat = jnp.stack([_block_diag(tr(c_re), gpb), _block_diag(-tr(c_im), gpb)], axis=2)
    abar = jnp.concatenate([abar_re.reshape(2, 1, g * p), abar_im.reshape(2, 1, g * p)], axis=-1)
    return dict(bmat=bmat.astype(BF16), cmat=cmat.astype(BF16), abar=abar)


def _interleave_perm(nseq, rows):
    src = np.arange(nseq * rows)
    dst = (src % rows) * nseq + src // rows
    pm = np.zeros((nseq * rows, nseq * rows), np.float32)
    pm[dst, src] = 1.0
    return pm


def _cmul_add(ar, ai, xr, xi, br, bi):
    return ar * xr - ai * xi + br, ar * xi + ai * xr + bi


def _permute_rows_exact(pm, y):
    hi = y.astype(BF16)
    r1 = y - hi.astype(F32)
    mid = r1.astype(BF16)
    lo = (r1 - mid.astype(F32)).astype(BF16)
    return _dot(pm, hi) + _dot(pm, mid) + _dot(pm, lo)


def _s5_kernel(*refs, reverse, final, tiles_per_blk):
    if final:
        (u_ref, h0_ref, pm_ref, bm_ref, cm_ref, a_ref, pmt_ref, yf_ref, g_ref, d_ref, wg_ref, bg_ref,
         o_ref, ht_ref, xs_ref, xb_ref, st_ref) = refs
    else:
        (u_ref, h0_ref, pm_ref, bm_ref, cm_ref, a_ref,
         o_ref, ht_ref, xs_ref, xb_ref, st_ref) = refs
    nseq, tcb, wbr = u_ref.shape
    n_tiles, lt = xs_ref.shape[0] // 2, xs_ref.shape[2]
    sp = n_tiles * lt
    nb = bm_ref.shape[1]
    bw_in, bw_st = wbr // nb, sp // nb
    tpb = bw_st // lt
    re_cols = lambda t: slice(t * lt, (t + 1) * lt)
    im_cols = lambda t: slice(sp + t * lt, sp + (t + 1) * lt)

    @pl.when(pl.program_id(1) == 0)
    def _():
        st_ref[...] = h0_ref[...]

    u = u_ref[...].reshape(nseq * tcb, wbr)
    ub = _dot(pm_ref[...], u.astype(BF16)).astype(BF16)
    for n in range(nb):
        bu = _dot(ub[:, n * bw_in:(n + 1) * bw_in], bm_ref[0, n])
        for t in range(tpb):
            xs_ref[n * tpb + t] = bu[:, t * lt:(t + 1) * lt]
            xs_ref[n_tiles + n * tpb + t] = bu[:, bw_st + t * lt:bw_st + (t + 1) * lt]

    npairs = tcb // 2
    for q in range(n_tiles // tiles_per_blk):
        tiles = range(q * tiles_per_blk, (q + 1) * tiles_per_blk)
        ar = [jnp.broadcast_to(a_ref[0, :, re_cols(t)], (nseq, lt)) for t in tiles]
        ai = [jnp.broadcast_to(a_ref[0, :, im_cols(t)], (nseq, lt)) for t in tiles]

        def step(i, st):
            jj = npairs - 1 - i if reverse else i
            r0 = pl.multiple_of(jj * 2 * nseq, 2 * nseq)
            out = []
            for idx, t in enumerate(tiles):
                dr, di = xs_ref[t, pl.ds(r0, 2 * nseq), :], xs_ref[n_tiles + t, pl.ds(r0, 2 * nseq), :]
                first, second = (slice(nseq, None), slice(0, nseq)) if reverse else (slice(0, nseq), slice(nseq, None))
                x0 = _cmul_add(ar[idx], ai[idx], st[2 * idx], st[2 * idx + 1], dr[first], di[first])
                x1 = _cmul_add(ar[idx], ai[idx], x0[0], x0[1], dr[second], di[second])
                lo, hi = (x1, x0) if reverse else (x0, x1)
                xb_ref[pl.ds(r0, 2 * nseq), re_cols(t)] = jnp.concatenate([lo[0], hi[0]], axis=0).astype(BF16)
                xb_ref[pl.ds(r0, 2 * nseq), im_cols(t)] = jnp.concatenate([lo[1], hi[1]], axis=0).astype(BF16)
                out += [x1[0], x1[1]]
            return tuple(out)

        init = []
        for t in tiles:
            init += [st_ref[:, re_cols(t)], st_ref[:, im_cols(t)]]
        last = lax.fori_loop(0, npairs, step, tuple(init), unroll=4)
        for idx, t in enumerate(tiles):
            st_ref[:, re_cols(t)] = last[2 * idx]
            st_ref[:, im_cols(t)] = last[2 * idx + 1]

    ht_ref[...] = st_ref[...]

    parts = []
    for n in range(nb):
        parts.append(_dot(xb_ref[:, n * bw_st:(n + 1) * bw_st], cm_ref[0, n, 0])
                     + _dot(xb_ref[:, sp + n * bw_st:sp + (n + 1) * bw_st], cm_ref[0, n, 1]))
    y = parts[0] if nb == 1 else jnp.concatenate(parts, axis=-1)

    if final:
        y = _permute_rows_exact(pmt_ref[...], yf_ref[...].reshape(nseq * tcb, wbr) + y)
        tot = jax.nn.gelu(y + d_ref[...] * u)
        glu = jax.nn.sigmoid(_dot(tot.astype(BF16), wg_ref[...]) + bg_ref[...])
        gate = _silu(g_ref[...].reshape(nseq * tcb, wbr))
        o_ref[...] = (tot * glu * gate).astype(BF16).reshape(nseq, tcb, wbr)
    else:
        o_ref[...] = y.reshape(nseq, tcb, wbr)


def _s5_pass(z3, h0, prm, dirn, wbr, u_blk, tcb, final_args=None):
    bsz, n, _ = z3.shape
    assert bsz % SUBLANES == 0 and n % tcb == 0 and tcb % 2 == 0
    nchunks = n // tcb
    reverse = dirn == 1
    final = final_args is not None
    sp2 = prm["abar"].shape[-1]
    bmat, cmat = prm["bmat"], prm["cmat"]
    sp = sp2 // 2
    lt = min(LANES, sp // bmat.shape[1])
    tiles_per_blk = min(sp // lt, 4)
    rows = SUBLANES * tcb
    pm = _interleave_perm(SUBLANES, tcb)
    chunk = (lambda k: nchunks - 1 - k) if reverse else (lambda k: k)
    seq_spec = lambda blk: pl.BlockSpec((SUBLANES, tcb, wbr), lambda g, k: (g, chunk(k), blk))
    const = lambda shape: pl.BlockSpec(shape, lambda g, k: (0,) * len(shape))
    dir_spec = lambda shape: pl.BlockSpec((1,) + shape[1:], lambda g, k: (dirn,) + (0,) * (len(shape) - 1))
    in_specs = [
        seq_spec(u_blk),
        pl.BlockSpec((SUBLANES, sp2), lambda g, k: (g, 0)),
        const((rows, rows)),
        dir_spec(bmat.shape), dir_spec(cmat.shape), dir_spec(prm["abar"].shape),
    ]
    args = [z3, h0, jnp.asarray(pm, BF16), bmat, cmat, prm["abar"]]
    if final:
        yf, gate_blk, d, w_glu, b_glu = final_args
        in_specs += [const((rows, rows)), seq_spec(0), seq_spec(gate_blk),
                     const((1, wbr)), const((wbr, wbr)), const((1, wbr))]
        args += [jnp.asarray(pm.T, BF16), yf, z3, d.reshape(1, wbr).astype(F32), w_glu.astype(BF16),
                 b_glu.reshape(1, wbr).astype(F32)]
    return pl.pallas_call(
        functools.partial(_s5_kernel, reverse=reverse, final=final, tiles_per_blk=tiles_per_blk),
        grid=(bsz // SUBLANES, nchunks),
        in_specs=in_specs,
        out_specs=[seq_spec(0), pl.BlockSpec((SUBLANES, sp2), lambda g, k: (g, 0))],
        out_shape=[
            jax.ShapeDtypeStruct((bsz, n, wbr), BF16 if final else F32),
            jax.ShapeDtypeStruct((bsz, sp2), F32),
        ],
        scratch_shapes=[
            pltpu.VMEM((sp2 // lt, rows, lt), F32),
            pltpu.VMEM((rows, sp2), BF16),
            pltpu.VMEM((SUBLANES, sp2), F32),
        ],
        compiler_params=_cparams(("parallel", "arbitrary")),
        name="s5_final" if final else "s5_fwd",
    )(*args)


def _s5(z3, h0, prm, d, w_glu, b_glu, wbr, u_blk, gate_blk, tcb):
    yf, hf = _s5_pass(z3, h0[0], prm, 0, wbr, u_blk, tcb)
    out, hb = _s5_pass(z3, h0[1], prm, 1, wbr, u_blk, tcb, final_args=(yf, gate_blk, d, w_glu, b_glu))
    return out, jnp.stack([hf, hb])


def _merge_kernel(h_ref, y0, y1, y2, y3, w0, w1, w2, w3, bm_ref, u0, u1, u2, u3, o_ref):
    h = h_ref[...]
    acc = None
    for i, (y, w, u) in enumerate(zip((y0, y1, y2, y3), (w0, w1, w2, w3), (u0, u1, u2, u3))):
        term = jax.nn.sigmoid(_dot(h, w[...]) + bm_ref[i]) * _dot(y[...], u[0])
        acc = term if acc is None else acc + term
    o_ref[...] = acc.astype(BF16)


def _merge(h2, ys, w_m, b_m, w_up, tm, tn):
    t, d = h2.shape
    wbr = ys[0].shape[1]
    ns = d // tn
    y_spec = pl.BlockSpec((tm, wbr), lambda i, s: (i, 0))
    w_specs = [pl.BlockSpec((d, tn), functools.partial(lambda i, s, br: (0, br * ns + s), br=br))
               for br in range(N_BRANCH)]
    u_specs = [pl.BlockSpec((1, wbr, tn), functools.partial(lambda i, s, br: (br, 0, s), br=br))
               for br in range(N_BRANCH)]
    return pl.pallas_call(
        _merge_kernel,
        grid=(t // tm, ns),
        in_specs=[
            pl.BlockSpec((tm, d), lambda i, s: (i, 0)),
            y_spec, y_spec, y_spec, y_spec,
            *w_specs,
            pl.BlockSpec((N_BRANCH, 1, tn), lambda i, s: (0, 0, s)),
            *u_specs,
        ],
        out_specs=pl.BlockSpec((tm, tn), lambda i, s: (i, s)),
        out_shape=jax.ShapeDtypeStruct((t, d), BF16),
        compiler_params=_cparams(("parallel", "arbitrary")),
        name="merge",
    )(h2, *ys, w_m, w_m, w_m, w_m, b_m, w_up, w_up, w_up, w_up)


def _outproj_kernel(m_ref, x_ref, mod_ref, w_ref, g_ref, b_ref, o_ref, *, d, alpha, sub):
    gate = mod_ref[0, :, 2 * d:3 * d]
    for r in range(m_ref.shape[0] // sub):
        rows = slice(r * sub, (r + 1) * sub)
        out = _dot(m_ref[rows, :], w_ref[...])
        o_ref[rows, :] = _ln(alpha * x_ref[rows, :] + gate * out) * g_ref[...] + b_ref[...]


def _out_proj(m2, x2, mod, w_o, ln_g, ln_b, alpha, tm):
    t, d = x2.shape
    tiles_per_mod = (t // mod.shape[0]) // tm
    row = pl.BlockSpec((tm, d), lambda i: (i, 0))
    vec = pl.BlockSpec((1, d), lambda i: (0, 0))
    return pl.pallas_call(
        functools.partial(_outproj_kernel, d=d, alpha=alpha, sub=_pick_tile(tm, 128)),
        grid=(t // tm,),
        in_specs=[row, row, _mod_spec(d, tiles_per_mod), pl.BlockSpec((d, d), lambda i: (0, 0)), vec, vec],
        out_specs=row,
        out_shape=jax.ShapeDtypeStruct((t, d), F32),
        compiler_params=_cparams(("parallel",)),
        name="out_proj",
    )(m2, x2, mod, w_o, ln_g.reshape(1, d), ln_b.reshape(1, d))


def _outmod_kernel(m_ref, x_ref, mod_ref, modn_ref, wo_ref, g_ref, b_ref, w_ref, bw_ref, xn_ref, h_ref, z_ref,
                   *, d, alpha, sub):
    gate = mod_ref[0, :, 2 * d:3 * d]
    nsub = m_ref.shape[0] // sub
    rows = lambda r: slice(r * sub, (r + 1) * sub)
    out = _dot(m_ref[rows(0), :], wo_ref[...])
    for r in range(nsub):
        nxt = _dot(m_ref[rows(r + 1), :], wo_ref[...]) if r + 1 < nsub else None
        xn = _ln(alpha * x_ref[rows(r), :] + gate * out) * g_ref[...] + b_ref[...]
        xn_ref[rows(r), :] = xn
        h = _modulated(xn, modn_ref, d).astype(BF16)
        h_ref[rows(r), :] = h
        z_ref[rows(r), :] = (_dot(h, w_ref[...]) + bw_ref[...]).astype(z_ref.dtype)
        out = nxt


def _out_mod_proj(m2, x2, mod, mod_next, w_o, ln_g, ln_b, alpha, w, b, ncols, tm):
    t, d = x2.shape
    tiles_per_mod = (t // mod.shape[0]) // tm
    row = lambda width: pl.BlockSpec((tm, width), lambda i: (i, 0))
    vec = lambda width: pl.BlockSpec((1, width), lambda i: (0, 0))
    resident = lambda shape: pl.BlockSpec(shape, lambda i: (0, 0), pipeline_mode=pl.Buffered(1))
    return pl.pallas_call(
        functools.partial(_outmod_kernel, d=d, alpha=alpha, sub=_pick_tile(tm, 128)),
        grid=(t // tm,),
        in_specs=[row(d), row(d), _mod_spec(d, tiles_per_mod), _mod_spec(d, tiles_per_mod),
                  resident((d, d)), vec(d), vec(d), resident((d, ncols)), vec(ncols)],
        out_specs=[row(d), row(d), row(ncols)],
        out_shape=[jax.ShapeDtypeStruct((t, d), F32), jax.ShapeDtypeStruct((t, d), BF16),
                   jax.ShapeDtypeStruct((t, ncols), BF16)],
        compiler_params=_cparams(("parallel",)),
        name="out_mod_proj",
    )(m2, x2, mod, mod_next, w_o, ln_g.reshape(1, d), ln_b.reshape(1, d), w, b)


def kernel(x, c, ctx, c_ctx, w_ada, b_ada, w_in, b_in, w_sgu, b_sgu, w_fnet, b_fnet, rpb, s5_a_re, s5_a_im,
           s5_log_dt, s5_b_re, s5_b_im, s5_c_re, s5_c_im, s5_d, s5_w_glu, s5_b_glu, w_up, w_o, ln_g, ln_b):
    bsz, n, d = x.shape
    lc = ctx.shape[1]
    depth = w_ada.shape[0]
    wbr = d // N_BRANCH
    heads = rpb.shape[1]
    win_h, win_w = (rpb.shape[2] + 1) // 2, (rpb.shape[3] + 1) // 2
    rows = n // GRID_W
    alpha = (2 * depth) ** 0.25
    sp2 = 2 * s5_a_re.shape[2] * s5_a_re.shape[3]

    W = wbr
    take = lambda a, spans: jnp.concatenate([a[..., lo * W:hi * W] for lo, hi in spans], axis=-1)
    spans16 = ((1, 4), (6, 7))
    spans32 = ((4, 6), (0, 1), (7, 11))
    off_merge = 11 * W
    blk_k, blk_v, blk_q, blk_b = 0, 1, 2, 3
    blk_uv, blk_u, blk_gate = 0, 2, 3
    n16, n32 = 4 * W, 7 * W

    n_rows = -(-(bsz + 1) // 8) * 8
    cond = jnp.zeros((n_rows, d), F32).at[:bsz].set(c).at[bsz].set(c_ctx)
    ada = _ada_all(cond, w_ada, b_ada)
    mod_x = [ada[l, :bsz].reshape(bsz, 1, 3 * d) for l in range(depth)]
    mod_c = [ada[l, bsz:bsz + 1].reshape(1, 1, 3 * d) for l in range(depth)]

    w16 = take(w_in, spans16).astype(BF16)
    w32 = take(w_in, spans32).astype(BF16)
    w_m = w_in[:, :, off_merge:].astype(BF16)
    w_up_b = w_up.astype(BF16)
    w_o_b = w_o.astype(BF16)

    plan = _nattn_plan(rows, GRID_W, win_h, win_w)
    gd_f = w_fnet.shape[2]
    cc, sc = _dft_tables(gd_f)
    dft_x = _dft_tables(n) + (cc, sc)
    dft_c = _dft_tables(lc) + (cc, sc)

    tcb_x, tcb_c = _pick_tile(n, S5_ROWS), _pick_tile(lc, S5_ROWS)
    tx, tc_all = bsz * n, bsz * lc
    tm_x, tm_c = _pick_tile(n, TM_MATMUL), _pick_tile(tc_all, TM_MATMUL)
    tm_o_x, tm_o_c = _pick_tile(n, TM_OUT), _pick_tile(tc_all, TM_OUT)
    tn32 = n32 // 2 if (n32 // 2) % LANES == 0 else W

    xt = x.reshape(tx, d)
    ct = ctx.reshape(tc_all, d)
    b16 = [take(b_in[l], spans16).reshape(1, n16) for l in range(depth)]
    ctx_cols16 = lambda l: n16 if l < depth - 1 else 2 * W
    zxb, hx = _mod_proj(xt, mod_x[0], w16[0], b16[0], n16, tm_o_x, BF16)
    zcb, hc = _mod_proj(ct, mod_c[0], w16[0], b16[0], ctx_cols16(0), tm_o_c, BF16)
    for l in range(depth):
        need_ctx = l < depth - 1
        b32 = take(b_in[l], spans32).reshape(1, n32)
        b_m = b_in[l, off_merge:].reshape(N_BRANCH, 1, d)

        zxb = zxb.reshape(bsz, n, n16)
        zxf = _in_proj(hx, w32[l], b32, n32, tm_x, tn32, F32).reshape(bsz, n, n32)
        c16, c32 = (n16, n32) if need_ctx else (2 * W, 3 * W)
        zcb = zcb.reshape(bsz, lc, c16)
        zcf = _in_proj(hc, w32[l], b32, c32, tm_c, tn32 if need_ctx else W, F32).reshape(bsz, lc, c32)

        prm = _s5_params(s5_a_re[l], s5_a_im[l], s5_log_dt[l], s5_b_re[l], s5_b_im[l], s5_c_re[l], s5_c_im[l])
        gate_c = blk_gate + 3 if need_ctx else blk_u
        yd_c, h_ctx = _s5(zcf, jnp.zeros((2, bsz, sp2), F32), prm, s5_d[l], s5_w_glu[l], s5_b_glu[l],
                          W, blk_u, gate_c, tcb_c)
        yd_x, _ = _s5(zxf, h_ctx, prm, s5_d[l], s5_w_glu[l], s5_b_glu[l], W, blk_u, blk_gate + 3, tcb_x)

        bias = _nattn_bias(rpb[l], plan, GRID_W)
        yc_x = _nattn(zxb, zcb, zxf, bias, plan, heads, W, (blk_k, blk_v, blk_q), blk_gate + 2)
        ya_x = _sgu(zxf, w_sgu[l], b_sgu[l], W, blk_uv, blk_gate + 0, _pick_tile(n, T_SGU))
        yb_x = _fnet(zxb, zxf, w_fnet[l], b_fnet[l], dft_x, W, blk_b, blk_gate + 1, _pick_tile(n, T_FNET))

        ys = [y.reshape(tx, W) for y in (ya_x, yb_x, yc_x, yd_x)]
        m = _merge(hx, ys, w_m[l], b_m, w_up_b[l], tm_x, _pick_tile(d, TN_MERGE))
        if need_ctx:
            xt, hx, zxb = _out_mod_proj(m, xt, mod_x[l], mod_x[l + 1], w_o_b[l], ln_g[l], ln_b[l], alpha,
                                        w16[l + 1], b16[l + 1], n16, tm_o_x)
        else:
            xt = _out_proj(m, xt, mod_x[l], w_o_b[l], ln_g[l], ln_b[l], alpha, tm_o_x)

        if need_ctx:
            yc_c = _cattn(zcb, zcf, heads, W, (blk_k, blk_v, blk_q), blk_gate + 2)
            ya_c = _sgu(zcf, w_sgu[l], b_sgu[l], W, blk_uv, blk_gate + 0, _pick_tile(lc, T_SGU))
            yb_c = _fnet(zcb, zcf, w_fnet[l], b_fnet[l], dft_c, W, blk_b, blk_gate + 1, _pick_tile(lc, T_FNET))
            ys_c = [y.reshape(tc_all, W) for y in (ya_c, yb_c, yc_c, yd_c)]
            m_c = _merge(hc, ys_c, w_m[l], b_m, w_up_b[l], tm_c, _pick_tile(d, TN_MERGE))
            ct, hc, zcb = _out_mod_proj(m_c, ct, mod_c[l], mod_c[l + 1], w_o_b[l], ln_g[l], ln_b[l], alpha,
                                        w16[l + 1], b16[l + 1], ctx_cols16(l + 1), tm_o_c)
    return xt.reshape(bsz, n, d)
```

```python
import functools
import math

import jax
import jax.numpy as jnp
import numpy as np
from jax import lax
from jax.experimental import pallas as pl
from jax.experimental.pallas import tpu as pltpu

GRID_W = 64
N_BRANCH = 4
LN_EPS = 1e-6
NEG_INF = -1e30
Q_ROWS = 4
SUBLANES = 8
LANES = 128
S5_ROWS = 64
TM_MATMUL = 1024
TM_OUT = 512
T_SGU = 512
T_FNET = 1024
TN_MERGE = 512
VMEM_LIMIT = 56 * 1024 * 1024

F32 = jnp.float32
BF16 = jnp.bfloat16
HIGHEST = lax.Precision.HIGHEST


def _cparams(sem):
    return pltpu.CompilerParams(dimension_semantics=sem, vmem_limit_bytes=VMEM_LIMIT)


def _dot(a, b):
    return jnp.dot(a, b, preferred_element_type=F32)


def _dot_t(a, b):
    return lax.dot_general(a, b, (((1,), (1,)), ((), ())), preferred_element_type=F32)


def _ln(x):
    mu = jnp.mean(x, axis=-1, keepdims=True)
    xc = x - mu
    var = jnp.mean(xc * xc, axis=-1, keepdims=True)
    return xc * lax.rsqrt(var + LN_EPS)


def _silu(x):
    return x * jax.nn.sigmoid(x)


def _pick_tile(n, target):
    t = min(n, target)
    while n % t:
        t //= 2
    return t


def _mod_spec(d, tiles_per_mod):
    return pl.BlockSpec((1, 1, 3 * d), lambda i, *_: (i // tiles_per_mod, 0, 0))


def _ada_kernel(s_ref, w_ref, b_ref, o_ref):
    s = _silu(s_ref[...]).astype(BF16)
    o_ref[0] = _dot(s, w_ref[0].astype(BF16)) + b_ref[0]


def _ada_all(cond, w_ada, b_ada):
    depth, d, d3 = w_ada.shape
    r = cond.shape[0]
    tn = _pick_tile(d3, 1024)
    return pl.pallas_call(
        _ada_kernel,
        grid=(depth, d3 // tn),
        in_specs=[
            pl.BlockSpec((r, d), lambda l, j: (0, 0)),
            pl.BlockSpec((1, d, tn), lambda l, j: (l, 0, j)),
            pl.BlockSpec((1, 1, tn), lambda l, j: (l, 0, j)),
        ],
        out_specs=pl.BlockSpec((1, r, tn), lambda l, j: (l, 0, j)),
        out_shape=jax.ShapeDtypeStruct((depth, r, d3), F32),
        compiler_params=_cparams(("arbitrary", "arbitrary")),
        name="ada_proj",
    )(cond, w_ada, b_ada.reshape(depth, 1, d3))


def _modulated(x, mod_ref, d):
    return _ln(x) * (1.0 + mod_ref[0, :, d:2 * d]) + mod_ref[0, :, 0:d]


def _modproj_kernel(x_ref, mod_ref, w_ref, b_ref, o_ref, h_ref, *, d, sub):
    for r in range(x_ref.shape[0] // sub):
        rows = slice(r * sub, (r + 1) * sub)
        h = _modulated(x_ref[rows, :], mod_ref, d).astype(BF16)
        h_ref[rows, :] = h
        o_ref[rows, :] = (_dot(h, w_ref[...]) + b_ref[...]).astype(o_ref.dtype)


def _mod_proj(x2, mod, w, b, ncols, tm, dtype):
    t, d = x2.shape
    tiles_per_mod = (t // mod.shape[0]) // tm
    return pl.pallas_call(
        functools.partial(_modproj_kernel, d=d, sub=_pick_tile(tm, 128)),
        grid=(t // tm,),
        in_specs=[
            pl.BlockSpec((tm, d), lambda i: (i, 0)),
            _mod_spec(d, tiles_per_mod),
            pl.BlockSpec((d, ncols), lambda i: (0, 0)),
            pl.BlockSpec((1, ncols), lambda i: (0, 0)),
        ],
        out_specs=[pl.BlockSpec((tm, ncols), lambda i: (i, 0)), pl.BlockSpec((tm, d), lambda i: (i, 0))],
        out_shape=[jax.ShapeDtypeStruct((t, ncols), dtype), jax.ShapeDtypeStruct((t, d), BF16)],
        compiler_params=_cparams(("parallel",)),
        name="mod_proj",
    )(x2, mod, w, b)


def _inproj_kernel(h_ref, w_ref, b_ref, o_ref):
    o_ref[...] = (_dot(h_ref[...], w_ref[...]) + b_ref[...]).astype(o_ref.dtype)


def _in_proj(h2, w, b, ncols, tm, tn, dtype):
    t, d = h2.shape
    return pl.pallas_call(
        _inproj_kernel,
        grid=(t // tm, ncols // tn),
        in_specs=[
            pl.BlockSpec((tm, d), lambda i, j: (i, 0)),
            pl.BlockSpec((d, tn), lambda i, j: (0, j)),
            pl.BlockSpec((1, tn), lambda i, j: (0, j)),
        ],
        out_specs=pl.BlockSpec((tm, tn), lambda i, j: (i, j)),
        out_shape=jax.ShapeDtypeStruct((t, ncols), dtype),
        compiler_params=_cparams(("parallel", "arbitrary")),
        name="in_proj",
    )(h2, w, b)


def _sgu_kernel(uv_ref, g_ref, w_ref, b_ref, o_ref, *, wbr, chunk, groups):
    uv = jax.nn.gelu(uv_ref[0])
    u = uv[:, :wbr]
    v = _ln(uv[:, wbr:]).astype(BF16)
    gate = _silu(g_ref[0])
    gd = wbr // groups
    for k in range(uv.shape[0] // chunk):
        rows = slice(k * chunk, (k + 1) * chunk)
        for g in range(groups):
            cols = slice(g * gd, (g + 1) * gd)
            mixed = _dot(w_ref[g], v[rows, cols]) + b_ref[g]
            o_ref[0, rows, cols] = (u[rows, cols] * mixed * gate[rows, cols]).astype(BF16)


def _sgu(z3, w_sgu, b_sgu, wbr, uv_blk, gate_blk, ta):
    bsz, n, _ = z3.shape
    groups, chunk, _ = w_sgu.shape
    return pl.pallas_call(
        functools.partial(_sgu_kernel, wbr=wbr, chunk=chunk, groups=groups),
        grid=(bsz, n // ta),
        in_specs=[
            pl.BlockSpec((1, ta, 2 * wbr), lambda b, i: (b, i, uv_blk)),
            pl.BlockSpec((1, ta, wbr), lambda b, i: (b, i, gate_blk)),
            pl.BlockSpec((groups, chunk, chunk), lambda b, i: (0, 0, 0)),
            pl.BlockSpec((groups, chunk, 1), lambda b, i: (0, 0, 0)),
        ],
        out_specs=pl.BlockSpec((1, ta, wbr), lambda b, i: (b, i, 0)),
        out_shape=jax.ShapeDtypeStruct((bsz, n, wbr), BF16),
        compiler_params=_cparams(("parallel", "arbitrary")),
        name="sgu_branch",
    )(z3, z3, w_sgu.astype(BF16), b_sgu.reshape(groups, chunk, 1))


def _dft_tables(n):
    k = jnp.arange(n, dtype=jnp.int32)
    r = (k[:, None] * k[None, :]) % n
    ang = r.astype(F32) * (2.0 * math.pi / n)
    return jnp.cos(ang).astype(BF16), jnp.sin(ang).astype(BF16)


def _fnet_kernel(z_ref, g_ref, cc_ref, sc_ref, cn_ref, sn_ref, wf_ref, bf_ref, o_ref, p_ref, q_ref,
                 *, groups, norm):
    wbr = z_ref.shape[2]
    gd = wbr // groups

    @pl.when(pl.program_id(1) == 0)
    def _():
        for g in range(groups):
            cols = slice(g * gd, (g + 1) * gd)
            zg = z_ref[0, :, cols]
            p_ref[:, cols] = _dot(zg, cc_ref[...]).astype(BF16)
            q_ref[:, cols] = _dot(zg, sc_ref[...]).astype(BF16)

    f = (_dot(cn_ref[...], p_ref[...]) - _dot(sn_ref[...], q_ref[...])) * norm
    gate = _silu(g_ref[0])
    for g in range(groups):
        cols = slice(g * gd, (g + 1) * gd)
        y = _dot(f[:, cols].astype(BF16), wf_ref[g]) + bf_ref[:, cols]
        o_ref[0, :, cols] = (y * gate[:, cols]).astype(BF16)


def _fnet(zb3, zf3, w_f, b_f, tables, wbr, z_blk, gate_blk, tk):
    bsz, n, _ = zb3.shape
    groups, gd, _ = w_f.shape
    cn, sn, cc, sc = tables
    return pl.pallas_call(
        functools.partial(_fnet_kernel, groups=groups, norm=1.0 / math.sqrt(n * gd)),
        grid=(bsz, n // tk),
        in_specs=[
            pl.BlockSpec((1, n, wbr), lambda b, k: (b, 0, z_blk)),
            pl.BlockSpec((1, tk, wbr), lambda b, k: (b, k, gate_blk)),
            pl.BlockSpec((gd, gd), lambda b, k: (0, 0)),
            pl.BlockSpec((gd, gd), lambda b, k: (0, 0)),
            pl.BlockSpec((tk, n), lambda b, k: (k, 0)),
            pl.BlockSpec((tk, n), lambda b, k: (k, 0)),
            pl.BlockSpec((groups, gd, gd), lambda b, k: (0, 0, 0)),
            pl.BlockSpec((1, wbr), lambda b, k: (0, 0)),
        ],
        out_specs=pl.BlockSpec((1, tk, wbr), lambda b, k: (b, k, 0)),
        out_shape=jax.ShapeDtypeStruct((bsz, n, wbr), BF16),
        scratch_shapes=[pltpu.VMEM((n, wbr), BF16), pltpu.VMEM((n, wbr), BF16)],
        compiler_params=_cparams(("parallel", "arbitrary")),
        name="fnet_branch",
    )(zb3, zf3, cc, sc, cn, sn, w_f.astype(BF16), b_f.reshape(1, wbr))


def _nattn_plan(rows, width, win_h, win_w):
    kh, kw = min(win_h, rows), min(win_w, width)
    qb = Q_ROWS
    wr = min(kh + qb - 1, rows)
    nblk = rows // qb
    r0 = np.arange(nblk) * qb
    ks = np.clip(r0 - kh // 2, 0, rows - wr)
    r = r0[:, None] + np.arange(qb)[None]
    rs = np.clip(r - kh // 2, 0, rows - kh)
    kr = ks[:, None] + np.arange(wr)[None]
    row_ok = (kr[:, None, :] >= rs[:, :, None]) & (kr[:, None, :] < rs[:, :, None] + kh)
    assert row_ok.sum(-1).min() == kh
    dri = np.where(row_ok, kr[:, None, :] - r[:, :, None] + win_h - 1, 0)
    key = np.concatenate([dri.reshape(nblk, -1), row_ok.reshape(nblk, -1)], axis=1)
    _, first, inv = np.unique(key, axis=0, return_index=True, return_inverse=True)
    qc = np.arange(width)
    col_start = np.clip(qc - kw // 2, 0, width - kw)
    in_win = (qc[None, :] >= col_start[:, None]) & (qc[None, :] < col_start[:, None] + kw)
    dci = np.clip(qc[None, :] - qc[:, None], -(kw - 1), kw - 1) + win_w - 1
    return dict(qb=qb, wr=wr, nblk=nblk, ks=ks.astype(np.int32), pat=inv.reshape(-1).astype(np.int32),
                dri=dri[first], row_ok=row_ok[first], in_win=in_win, dci=dci)


def _nattn_bias(rpb, plan, width):
    dri, row_ok = plan["dri"], plan["row_ok"]
    npat, qb, wr = dri.shape
    heads, _, n_dc = rpb.shape
    rows = jnp.take(rpb.astype(F32), jnp.asarray(dri.reshape(-1)), axis=1)
    onehot = (plan["dci"].reshape(-1)[None, :] == np.arange(n_dc)[:, None]).astype(np.float32)
    vals = jnp.einsum("hrc,cw->hrw", rows, jnp.asarray(onehot), precision=HIGHEST)
    vals = vals.reshape(heads, npat, qb, wr, width, width).transpose(1, 0, 2, 4, 3, 5)
    valid = row_ok[:, None, :, None, :, None] & plan["in_win"][None, None, None, :, None, :]
    vals = jnp.where(jnp.asarray(valid), vals, NEG_INF)
    return vals.reshape(npat, heads, qb * width, wr * width)


def _softmax_pv(s_band, s_ctx, v_band, v_ctx):
    m = jnp.maximum(jnp.max(s_band, axis=-1, keepdims=True), jnp.max(s_ctx, axis=-1, keepdims=True))
    p_band = jnp.exp(s_band - m)
    p_ctx = jnp.exp(s_ctx - m)
    den = jnp.sum(p_band, axis=-1, keepdims=True) + jnp.sum(p_ctx, axis=-1, keepdims=True)
    out = _dot(p_band.astype(BF16), v_band) + _dot(p_ctx.astype(BF16), v_ctx)
    return out / den


def _nattn_kernel(ks_ref, pat_ref, q_ref, k_ref, v_ref, ck_ref, cv_ref, g_ref, bias_ref, o_ref,
                  *, heads, width, nkeys):
    blk = pl.program_id(1)
    start = pl.multiple_of(ks_ref[blk] * width, width)
    pat = pat_ref[blk]
    dh = q_ref.shape[2] // heads
    scale = dh ** -0.5
    gate = _silu(g_ref[0])
    head_cols = lambda h: slice(h * dh, (h + 1) * dh)

    def scores(h):
        cols = head_cols(h)
        kb = k_ref[0, pl.ds(start, nkeys), cols]
        s_band = _dot_t(q_ref[0, :, cols], kb) * scale + bias_ref[pat, h]
        s_ctx = _dot_t(q_ref[0, :, cols], ck_ref[0, :, cols]) * scale
        return s_band, s_ctx

    s = scores(0)
    for h in range(heads):
        nxt = scores(h + 1) if h + 1 < heads else None
        cols = head_cols(h)
        vb = v_ref[0, pl.ds(start, nkeys), cols]
        out = _softmax_pv(s[0], s[1], vb, cv_ref[0, :, cols])
        o_ref[0, :, cols] = (out * gate[:, cols]).astype(BF16)
        s = nxt


def _nattn(zxb, zcb, zxf, bias, plan, heads, wbr, blks, gate_blk):
    bsz, n, _ = zxb.shape
    lc = zcb.shape[1]
    k_blk, v_blk, q_blk = blks
    nq = plan["qb"] * GRID_W
    nkeys = plan["wr"] * GRID_W
    grid_spec = pltpu.PrefetchScalarGridSpec(
        num_scalar_prefetch=2,
        grid=(bsz, plan["nblk"]),
        in_specs=[
            pl.BlockSpec((1, nq, wbr), lambda b, i, *_: (b, i, q_blk)),
            pl.BlockSpec((1, n, wbr), lambda b, i, *_: (b, 0, k_blk)),
            pl.BlockSpec((1, n, wbr), lambda b, i, *_: (b, 0, v_blk)),
            pl.BlockSpec((1, lc, wbr), lambda b, i, *_: (b, 0, k_blk)),
            pl.BlockSpec((1, lc, wbr), lambda b, i, *_: (b, 0, v_blk)),
            pl.BlockSpec((1, nq, wbr), lambda b, i, *_: (b, i, gate_blk)),
            pl.BlockSpec(bias.shape, lambda b, i, *_: (0, 0, 0, 0)),
        ],
        out_specs=pl.BlockSpec((1, nq, wbr), lambda b, i, *_: (b, i, 0)),
    )
    return pl.pallas_call(
        functools.partial(_nattn_kernel, heads=heads, width=GRID_W, nkeys=nkeys),
        grid_spec=grid_spec,
        out_shape=jax.ShapeDtypeStruct((bsz, n, wbr), BF16),
        compiler_params=_cparams(("parallel", "arbitrary")),
        name="nattn_branch",
    )(jnp.asarray(plan["ks"]), jnp.asarray(plan["pat"]), zxb, zxb, zxb, zcb, zcb, zxf, bias)


def _cattn_kernel(q_ref, k_ref, v_ref, g_ref, o_ref, *, heads):
    dh = q_ref.shape[2] // heads
    scale = dh ** -0.5
    gate = _silu(g_ref[0])
    for h in range(heads):
        cols = slice(h * dh, (h + 1) * dh)
        s = _dot_t(q_ref[0, :, cols], k_ref[0, :, cols]) * scale
        p = jnp.exp(s - jnp.max(s, axis=-1, keepdims=True))
        out = _dot(p.astype(BF16), v_ref[0, :, cols]) / jnp.sum(p, axis=-1, keepdims=True)
        o_ref[0, :, cols] = (out * gate[:, cols]).astype(BF16)


def _cattn(zcb, zcf, heads, wbr, blks, gate_blk):
    bsz, lc, _ = zcb.shape
    k_blk, v_blk, q_blk = blks
    spec = lambda blk: pl.BlockSpec((1, lc, wbr), lambda b: (b, 0, blk))
    return pl.pallas_call(
        functools.partial(_cattn_kernel, heads=heads),
        grid=(bsz,),
        in_specs=[spec(q_blk), spec(k_blk), spec(v_blk), spec(gate_blk)],
        out_specs=spec(0),
        out_shape=jax.ShapeDtypeStruct((bsz, lc, wbr), BF16),
        compiler_params=_cparams(("parallel",)),
        name="cattn_branch",
    )(zcb, zcb, zcb, zcf)


def _s5_discretise(a_re, a_im, log_dt, b_re, b_im):
    dt = jnp.exp(log_dt)[..., None]
    lr, li = a_re * dt, a_im * dt
    mag = jnp.exp(lr)
    abar_re, abar_im = mag * jnp.cos(li), mag * jnp.sin(li)
    den = a_re * a_re + a_im * a_im
    f_re = ((abar_re - 1.0) * a_re + abar_im * a_im) / den
    f_im = (abar_im * a_re - (abar_re - 1.0) * a_im) / den
    bb_re = f_re[..., None] * b_re - f_im[..., None] * b_im
    bb_im = f_re[..., None] * b_im + f_im[..., None] * b_re
    return abar_re, abar_im, bb_re, bb_im


def _block_diag(m, gpb):
    two, g, a, b = m.shape
    nb = g // gpb
    m = m.reshape(two, nb, gpb, a, b)
    eye = jnp.eye(gpb, dtype=m.dtype)
    out = m[:, :, :, :, None, :] * eye[None, None, :, None, :, None]
    return out.reshape(two, nb, gpb * a, gpb * b)


def _s5_params(a_re, a_im, log_dt, b_re, b_im, c_re, c_im):
    f32 = lambda t: t.astype(F32)
    a_re, a_im, log_dt, b_re, b_im, c_re, c_im = map(f32, (a_re, a_im, log_dt, b_re, b_im, c_re, c_im))
    _, g, p, h = b_re.shape
    gpb = min(g, max(1, LANES // h))
    abar_re, abar_im, bb_re, bb_im = _s5_discretise(a_re, a_im, log_dt, b_re, b_im)
    tr = lambda t: jnp.swapaxes(t, -1, -2)
    bmat = jnp.concatenate([_block_diag(tr(bb_re), gpb), _block_diag(tr(bb_im), gpb)], axis=-1)
    cmat = jnp.stack([_block_diag(tr(c_re), gpb), _block_diag(-tr(c_im), gpb)], axis=2)
    abar = jnp.concatenate([abar_re.reshape(2, 1, g * p), abar_im.reshape(2, 1, g * p)], axis=-1)
    return dict(bmat=bmat.astype(BF16), cmat=cmat.astype(BF16), abar=abar)


def _interleave_perm(nseq, rows):
    src = np.arange(nseq * rows)
    dst = (src % rows) * nseq + src // rows
    pm = np.zeros((nseq * rows, nseq * rows), np.float32)
    pm[dst, src] = 1.0
    return pm


def _cmul_add(ar, ai, xr, xi, br, bi):
    return ar * xr - ai * xi + br, ar * xi + ai * xr + bi


def _permute_rows_exact(pm, y):
    hi = y.astype(BF16)
    r1 = y - hi.astype(F32)
    mid = r1.astype(BF16)
    lo = (r1 - mid.astype(F32)).astype(BF16)
    return _dot(pm, hi) + _dot(pm, mid) + _dot(pm, lo)


def _s5_kernel(*refs, reverse, final):
    if final:
        (u_ref, h0_ref, pm_ref, bm_ref, cm_ref, a_ref, pmt_ref, yf_ref, g_ref, d_ref, wg_ref, bg_ref,
         o_ref, ht_ref, xs_ref, xb_ref, st_ref) = refs
    else:
        (u_ref, h0_ref, pm_ref, bm_ref, cm_ref, a_ref,
         o_ref, ht_ref, xs_ref, xb_ref, st_ref) = refs
    nseq, tcb, wbr = u_ref.shape
    n_tiles, lt = xs_ref.shape[0] // 2, xs_ref.shape[2]
    sp = n_tiles * lt
    nb = bm_ref.shape[1]
    bw_in, bw_st = wbr // nb, sp // nb
    tpb = bw_st // lt
    re_cols = lambda t: slice(t * lt, (t + 1) * lt)
    im_cols = lambda t: slice(sp + t * lt, sp + (t + 1) * lt)

    @pl.when(pl.program_id(1) == 0)
    def _():
        st_ref[...] = h0_ref[...]

    u = u_ref[...].reshape(nseq * tcb, wbr)
    ub = _dot(pm_ref[...], u.astype(BF16)).astype(BF16)
    def drive(n):
        bu = _dot(ub[:, n * bw_in:(n + 1) * bw_in], bm_ref[0, n])
        for t in range(tpb):
            xs_ref[n * tpb + t] = bu[:, t * lt:(t + 1) * lt]
            xs_ref[n_tiles + n * tpb + t] = bu[:, bw_st + t * lt:bw_st + (t + 1) * lt]

    npairs = tcb // 2
    parts = []
    drive(0)
    for q in range(nb):
        if q + 1 < nb:
            drive(q + 1)
        tiles = range(q * tpb, (q + 1) * tpb)
        ar = [jnp.broadcast_to(a_ref[0, :, re_cols(t)], (nseq, lt)) for t in tiles]
        ai = [jnp.broadcast_to(a_ref[0, :, im_cols(t)], (nseq, lt)) for t in tiles]

        def step(i, st):
            jj = npairs - 1 - i if reverse else i
            r0 = pl.multiple_of(jj * 2 * nseq, 2 * nseq)
            out = []
            for idx, t in enumerate(tiles):
                dr, di = xs_ref[t, pl.ds(r0, 2 * nseq), :], xs_ref[n_tiles + t, pl.ds(r0, 2 * nseq), :]
                first, second = (slice(nseq, None), slice(0, nseq)) if reverse else (slice(0, nseq), slice(nseq, None))
                x0 = _cmul_add(ar[idx], ai[idx], st[2 * idx], st[2 * idx + 1], dr[first], di[first])
                x1 = _cmul_add(ar[idx], ai[idx], x0[0], x0[1], dr[second], di[second])
                lo, hi = (x1, x0) if reverse else (x0, x1)
                xb_ref[pl.ds(r0, 2 * nseq), re_cols(t)] = jnp.concatenate([lo[0], hi[0]], axis=0).astype(BF16)
                xb_ref[pl.ds(r0, 2 * nseq), im_cols(t)] = jnp.concatenate([lo[1], hi[1]], axis=0).astype(BF16)
                out += [x1[0], x1[1]]
            return tuple(out)

        init = []
        for t in tiles:
            init += [st_ref[:, re_cols(t)], st_ref[:, im_cols(t)]]
        last = lax.fori_loop(0, npairs, step, tuple(init), unroll=True)
        for idx, t in enumerate(tiles):
            st_ref[:, re_cols(t)] = last[2 * idx]
            st_ref[:, im_cols(t)] = last[2 * idx + 1]
        parts.append(_dot(xb_ref[:, q * bw_st:(q + 1) * bw_st], cm_ref[0, q, 0])
                     + _dot(xb_ref[:, sp + q * bw_st:sp + (q + 1) * bw_st], cm_ref[0, q, 1]))

    ht_ref[...] = st_ref[...]
    y = parts[0] if nb == 1 else jnp.concatenate(parts, axis=-1)

    if final:
        y = _permute_rows_exact(pmt_ref[...], yf_ref[...].reshape(nseq * tcb, wbr) + y)
        tot = jax.nn.gelu(y + d_ref[...] * u)
        glu = jax.nn.sigmoid(_dot(tot.astype(BF16), wg_ref[...]) + bg_ref[...])
        gate = _silu(g_ref[...].reshape(nseq * tcb, wbr))
        o_ref[...] = (tot * glu * gate).astype(BF16).reshape(nseq, tcb, wbr)
    else:
        o_ref[...] = y.reshape(nseq, tcb, wbr)


def _s5_pass(z3, h0, prm, dirn, wbr, u_blk, tcb, final_args=None):
    bsz, n, _ = z3.shape
    assert bsz % SUBLANES == 0 and n % tcb == 0 and tcb % 2 == 0
    nchunks = n // tcb
    reverse = dirn == 1
    final = final_args is not None
    sp2 = prm["abar"].shape[-1]
    bmat, cmat = prm["bmat"], prm["cmat"]
    sp = sp2 // 2
    lt = min(LANES, sp // bmat.shape[1])
    rows = SUBLANES * tcb
    pm = _interleave_perm(SUBLANES, tcb)
    chunk = (lambda k: nchunks - 1 - k) if reverse else (lambda k: k)
    seq_spec = lambda blk: pl.BlockSpec((SUBLANES, tcb, wbr), lambda g, k: (g, chunk(k), blk))
    const = lambda shape: pl.BlockSpec(shape, lambda g, k: (0,) * len(shape))
    dir_spec = lambda shape: pl.BlockSpec((1,) + shape[1:], lambda g, k: (dirn,) + (0,) * (len(shape) - 1))
    in_specs = [
        seq_spec(u_blk),
        pl.BlockSpec((SUBLANES, sp2), lambda g, k: (g, 0)),
        const((rows, rows)),
        dir_spec(bmat.shape), dir_spec(cmat.shape), dir_spec(prm["abar"].shape),
    ]
    args = [z3, h0, jnp.asarray(pm, BF16), bmat, cmat, prm["abar"]]
    if final:
        yf, gate_blk, d, w_glu, b_glu = final_args
        in_specs += [const((rows, rows)), seq_spec(0), seq_spec(gate_blk),
                     const((1, wbr)), const((wbr, wbr)), const((1, wbr))]
        args += [jnp.asarray(pm.T, BF16), yf, z3, d.reshape(1, wbr).astype(F32), w_glu.astype(BF16),
                 b_glu.reshape(1, wbr).astype(F32)]
    return pl.pallas_call(
        functools.partial(_s5_kernel, reverse=reverse, final=final),
        grid=(bsz // SUBLANES, nchunks),
        in_specs=in_specs,
        out_specs=[seq_spec(0), pl.BlockSpec((SUBLANES, sp2), lambda g, k: (g, 0))],
        out_shape=[
            jax.ShapeDtypeStruct((bsz, n, wbr), BF16 if final else F32),
            jax.ShapeDtypeStruct((bsz, sp2), F32),
        ],
        scratch_shapes=[
            pltpu.VMEM((sp2 // lt, rows, lt), F32),
            pltpu.VMEM((rows, sp2), BF16),
            pltpu.VMEM((SUBLANES, sp2), F32),
        ],
        compiler_params=_cparams(("parallel", "arbitrary")),
        name="s5_final" if final else "s5_fwd",
    )(*args)


def _s5(z3, h0, prm, d, w_glu, b_glu, wbr, u_blk, gate_blk, tcb):
    yf, hf = _s5_pass(z3, h0[0], prm, 0, wbr, u_blk, tcb)
    out, hb = _s5_pass(z3, h0[1], prm, 1, wbr, u_blk, tcb, final_args=(yf, gate_blk, d, w_glu, b_glu))
    return out, jnp.stack([hf, hb])


def _merge_kernel(h_ref, y0, y1, y2, y3, w0, w1, w2, w3, bm_ref, u0, u1, u2, u3, o_ref):
    h = h_ref[...]
    acc = None
    for i, (y, w, u) in enumerate(zip((y0, y1, y2, y3), (w0, w1, w2, w3), (u0, u1, u2, u3))):
        term = jax.nn.sigmoid(_dot(h, w[...]) + bm_ref[i]) * _dot(y[...], u[0])
        acc = term if acc is None else acc + term
    o_ref[...] = acc.astype(BF16)


def _merge(h2, ys, w_m, b_m, w_up, tm, tn):
    t, d = h2.shape
    wbr = ys[0].shape[1]
    ns = d // tn
    y_spec = pl.BlockSpec((tm, wbr), lambda i, s: (i, 0))
    w_specs = [pl.BlockSpec((d, tn), functools.partial(lambda i, s, br: (0, br * ns + s), br=br))
               for br in range(N_BRANCH)]
    u_specs = [pl.BlockSpec((1, wbr, tn), functools.partial(lambda i, s, br: (br, 0, s), br=br))
               for br in range(N_BRANCH)]
    return pl.pallas_call(
        _merge_kernel,
        grid=(t // tm, ns),
        in_specs=[
            pl.BlockSpec((tm, d), lambda i, s: (i, 0)),
            y_spec, y_spec, y_spec, y_spec,
            *w_specs,
            pl.BlockSpec((N_BRANCH, 1, tn), lambda i, s: (0, 0, s)),
            *u_specs,
        ],
        out_specs=pl.BlockSpec((tm, tn), lambda i, s: (i, s)),
        out_shape=jax.ShapeDtypeStruct((t, d), BF16),
        compiler_params=_cparams(("parallel", "arbitrary")),
        name="merge",
    )(h2, *ys, w_m, w_m, w_m, w_m, b_m, w_up, w_up, w_up, w_up)


def _outproj_kernel(m_ref, x_ref, mod_ref, w_ref, g_ref, b_ref, o_ref, *, d, alpha, sub):
    gate = mod_ref[0, :, 2 * d:3 * d]
    for r in range(m_ref.shape[0] // sub):
        rows = slice(r * sub, (r + 1) * sub)
        out = _dot(m_ref[rows, :], w_ref[...])
        o_ref[rows, :] = _ln(alpha * x_ref[rows, :] + gate * out) * g_ref[...] + b_ref[...]


def _out_proj(m2, x2, mod, w_o, ln_g, ln_b, alpha, tm):
    t, d = x2.shape
    tiles_per_mod = (t // mod.shape[0]) // tm
    row = pl.BlockSpec((tm, d), lambda i: (i, 0))
    vec = pl.BlockSpec((1, d), lambda i: (0, 0))
    return pl.pallas_call(
        functools.partial(_outproj_kernel, d=d, alpha=alpha, sub=_pick_tile(tm, 128)),
        grid=(t // tm,),
        in_specs=[row, row, _mod_spec(d, tiles_per_mod), pl.BlockSpec((d, d), lambda i: (0, 0)), vec, vec],
        out_specs=row,
        out_shape=jax.ShapeDtypeStruct((t, d), F32),
        compiler_params=_cparams(("parallel",)),
        name="out_proj",
    )(m2, x2, mod, w_o, ln_g.reshape(1, d), ln_b.reshape(1, d))


def _outmod_kernel(m_ref, x_ref, mod_ref, modn_ref, wo_ref, g_ref, b_ref, w_ref, bw_ref, xn_ref, h_ref, z_ref,
                   *, d, alpha, sub):
    gate = mod_ref[0, :, 2 * d:3 * d]
    nsub = m_ref.shape[0] // sub
    rows = lambda r: slice(r * sub, (r + 1) * sub)
    out = _dot(m_ref[rows(0), :], wo_ref[...])
    for r in range(nsub):
        nxt = _dot(m_ref[rows(r + 1), :], wo_ref[...]) if r + 1 < nsub else None
        xn = _ln(alpha * x_ref[rows(r), :] + gate * out) * g_ref[...] + b_ref[...]
        xn_ref[rows(r), :] = xn
        h = _modulated(xn, modn_ref, d).astype(BF16)
        h_ref[rows(r), :] = h
        z_ref[rows(r), :] = (_dot(h, w_ref[...]) + bw_ref[...]).astype(z_ref.dtype)
        out = nxt


def _out_mod_proj(m2, x2, mod, mod_next, w_o, ln_g, ln_b, alpha, w, b, ncols, tm):
    t, d = x2.shape
    tiles_per_mod = (t // mod.shape[0]) // tm
    row = lambda width: pl.BlockSpec((tm, width), lambda i: (i, 0))
    vec = lambda width: pl.BlockSpec((1, width), lambda i: (0, 0))
    resident = lambda shape: pl.BlockSpec(shape, lambda i: (0, 0), pipeline_mode=pl.Buffered(1))
    return pl.pallas_call(
        functools.partial(_outmod_kernel, d=d, alpha=alpha, sub=_pick_tile(tm, 128)),
        grid=(t // tm,),
        in_specs=[row(d), row(d), _mod_spec(d, tiles_per_mod), _mod_spec(d, tiles_per_mod),
                  resident((d, d)), vec(d), vec(d), resident((d, ncols)), vec(ncols)],
        out_specs=[row(d), row(d), row(ncols)],
        out_shape=[jax.ShapeDtypeStruct((t, d), F32), jax.ShapeDtypeStruct((t, d), BF16),
                   jax.ShapeDtypeStruct((t, ncols), BF16)],
        compiler_params=_cparams(("parallel",)),
        name="out_mod_proj",
    )(m2, x2, mod, mod_next, w_o, ln_g.reshape(1, d), ln_b.reshape(1, d), w, b)


def kernel(x, c, ctx, c_ctx, w_ada, b_ada, w_in, b_in, w_sgu, b_sgu, w_fnet, b_fnet, rpb, s5_a_re, s5_a_im,
           s5_log_dt, s5_b_re, s5_b_im, s5_c_re, s5_c_im, s5_d, s5_w_glu, s5_b_glu, w_up, w_o, ln_g, ln_b):
    bsz, n, d = x.shape
    lc = ctx.shape[1]
    depth = w_ada.shape[0]
    wbr = d // N_BRANCH
    heads = rpb.shape[1]
    win_h, win_w = (rpb.shape[2] + 1) // 2, (rpb.shape[3] + 1) // 2
    rows = n // GRID_W
    alpha = (2 * depth) ** 0.25
    sp2 = 2 * s5_a_re.shape[2] * s5_a_re.shape[3]

    W = wbr
    take = lambda a, spans: jnp.concatenate([a[..., lo * W:hi * W] for lo, hi in spans], axis=-1)
    spans16 = ((1, 4), (6, 7))
    spans32 = ((4, 6), (0, 1), (7, 11))
    off_merge = 11 * W
    blk_k, blk_v, blk_q, blk_b = 0, 1, 2, 3
    blk_uv, blk_u, blk_gate = 0, 2, 3
    n16, n32 = 4 * W, 7 * W

    n_rows = -(-(bsz + 1) // 8) * 8
    cond = jnp.zeros((n_rows, d), F32).at[:bsz].set(c).at[bsz].set(c_ctx)
    ada = _ada_all(cond, w_ada, b_ada)
    mod_x = [ada[l, :bsz].reshape(bsz, 1, 3 * d) for l in range(depth)]
    mod_c = [ada[l, bsz:bsz + 1].reshape(1, 1, 3 * d) for l in range(depth)]

    w16 = take(w_in, spans16).astype(BF16)
    w32 = take(w_in, spans32).astype(BF16)
    w_m = w_in[:, :, off_merge:].astype(BF16)
    w_up_b = w_up.astype(BF16)
    w_o_b = w_o.astype(BF16)

    plan = _nattn_plan(rows, GRID_W, win_h, win_w)
    gd_f = w_fnet.shape[2]
    cc, sc = _dft_tables(gd_f)
    dft_x = _dft_tables(n) + (cc, sc)
    dft_c = _dft_tables(lc) + (cc, sc)

    tcb_x, tcb_c = _pick_tile(n, S5_ROWS), _pick_tile(lc, S5_ROWS)
    tx, tc_all = bsz * n, bsz * lc
    tm_x, tm_c = _pick_tile(n, TM_MATMUL), _pick_tile(tc_all, TM_MATMUL)
    tm_o_x, tm_o_c = _pick_tile(n, TM_OUT), _pick_tile(tc_all, TM_OUT)
    tn32 = n32 // 2 if (n32 // 2) % LANES == 0 else W

    xt = x.reshape(tx, d)
    ct = ctx.reshape(tc_all, d)
    b16 = [take(b_in[l], spans16).reshape(1, n16) for l in range(depth)]
    ctx_cols16 = lambda l: n16 if l < depth - 1 else 2 * W
    zxb, hx = _mod_proj(xt, mod_x[0], w16[0], b16[0], n16, tm_o_x, BF16)
    zcb, hc = _mod_proj(ct, mod_c[0], w16[0], b16[0], ctx_cols16(0), tm_o_c, BF16)
    for l in range(depth):
        need_ctx = l < depth - 1
        b32 = take(b_in[l], spans32).reshape(1, n32)
        b_m = b_in[l, off_merge:].reshape(N_BRANCH, 1, d)

        zxb = zxb.reshape(bsz, n, n16)
        zxf = _in_proj(hx, w32[l], b32, n32, tm_x, tn32, F32).reshape(bsz, n, n32)
        c16, c32 = (n16, n32) if need_ctx else (2 * W, 3 * W)
        zcb = zcb.reshape(bsz, lc, c16)
        zcf = _in_proj(hc, w32[l], b32, c32, tm_c, tn32 if need_ctx else W, F32).reshape(bsz, lc, c32)

        prm = _s5_params(s5_a_re[l], s5_a_im[l], s5_log_dt[l], s5_b_re[l], s5_b_im[l], s5_c_re[l], s5_c_im[l])
        gate_c = blk_gate + 3 if need_ctx else blk_u
        yd_c, h_ctx = _s5(zcf, jnp.zeros((2, bsz, sp2), F32), prm, s5_d[l], s5_w_glu[l], s5_b_glu[l],
                          W, blk_u, gate_c, tcb_c)
        yd_x, _ = _s5(zxf, h_ctx, prm, s5_d[l], s5_w_glu[l], s5_b_glu[l], W, blk_u, blk_gate + 3, tcb_x)

        bias = _nattn_bias(rpb[l], plan, GRID_W)
        yc_x = _nattn(zxb, zcb, zxf, bias, plan, heads, W, (blk_k, blk_v, blk_q), blk_gate + 2)
        ya_x = _sgu(zxf, w_sgu[l], b_sgu[l], W, blk_uv, blk_gate + 0, _pick_tile(n, T_SGU))
        yb_x = _fnet(zxb, zxf, w_fnet[l], b_fnet[l], dft_x, W, blk_b, blk_gate + 1, _pick_tile(n, T_FNET))

        ys = [y.reshape(tx, W) for y in (ya_x, yb_x, yc_x, yd_x)]
        m = _merge(hx, ys, w_m[l], b_m, w_up_b[l], tm_x, _pick_tile(d, TN_MERGE))
        if need_ctx:
            xt, hx, zxb = _out_mod_proj(m, xt, mod_x[l], mod_x[l + 1], w_o_b[l], ln_g[l], ln_b[l], alpha,
                                        w16[l + 1], b16[l + 1], n16, tm_o_x)
        else:
            xt = _out_proj(m, xt, mod_x[l], w_o_b[l], ln_g[l], ln_b[l], alpha, tm_o_x)

        if need_ctx:
            yc_c = _cattn(zcb, zcf, heads, W, (blk_k, blk_v, blk_q), blk_gate + 2)
            ya_c = _sgu(zcf, w_sgu[l], b_sgu[l], W, blk_uv, blk_gate + 0, _pick_tile(lc, T_SGU))
            yb_c = _fnet(zcb, zcf, w_fnet[l], b_fnet[l], dft_c, W, blk_b, blk_gate + 1, _pick_tile(lc, T_FNET))
            ys_c = [y.reshape(tc_all, W) for y in (ya_c, yb_c, yc_c, yd_c)]
            m_c = _merge(hc, ys_c, w_m[l], b_m, w_up_b[l], tm_c, _pick_tile(d, TN_MERGE))
            ct, hc, zcb = _out_mod_proj(m_c, ct, mod_c[l], mod_c[l + 1], w_o_b[l], ln_g[l], ln_b[l], alpha,
                                        w16[l + 1], b16[l + 1], ctx_cols16(l + 1), tm_o_c)
    return xt.reshape(bsz, n, d)
```

```python
import functools
import math

import jax
import jax.numpy as jnp
import numpy as np
from jax import lax
from jax.experimental import pallas as pl
from jax.experimental.pallas import tpu as pltpu

GRID_W = 64
N_BRANCH = 4
LN_EPS = 1e-6
NEG_INF = -1e30
Q_ROWS = 4
SUBLANES = 8
LANES = 128
S5_ROWS = 64
TM_MATMUL = 1024
TM_OUT = 512
T_SGU = 512
T_FNET = 1024
TN_MERGE = 512
VMEM_LIMIT = 56 * 1024 * 1024

F32 = jnp.float32
BF16 = jnp.bfloat16
HIGHEST = lax.Precision.HIGHEST


def _cparams(sem):
    return pltpu.CompilerParams(dimension_semantics=sem, vmem_limit_bytes=VMEM_LIMIT)


def _dot(a, b):
    return jnp.dot(a, b, preferred_element_type=F32)


def _dot_t(a, b):
    return lax.dot_general(a, b, (((1,), (1,)), ((), ())), preferred_element_type=F32)


def _ln(x):
    mu = jnp.mean(x, axis=-1, keepdims=True)
    xc = x - mu
    var = jnp.mean(xc * xc, axis=-1, keepdims=True)
    return xc * lax.rsqrt(var + LN_EPS)


def _silu(x):
    return x * jax.nn.sigmoid(x)


def _pick_tile(n, target):
    t = min(n, target)
    while n % t:
        t //= 2
    return t


def _mod_spec(d, tiles_per_mod):
    return pl.BlockSpec((1, 1, 3 * d), lambda i, *_: (i // tiles_per_mod, 0, 0))


def _ada_kernel(s_ref, w_ref, b_ref, o_ref):
    s = _silu(s_ref[...]).astype(BF16)
    o_ref[0] = _dot(s, w_ref[0].astype(BF16)) + b_ref[0]


def _ada_all(cond, w_ada, b_ada):
    depth, d, d3 = w_ada.shape
    r = cond.shape[0]
    tn = _pick_tile(d3, 1024)
    return pl.pallas_call(
        _ada_kernel,
        grid=(depth, d3 // tn),
        in_specs=[
            pl.BlockSpec((r, d), lambda l, j: (0, 0)),
            pl.BlockSpec((1, d, tn), lambda l, j: (l, 0, j)),
            pl.BlockSpec((1, 1, tn), lambda l, j: (l, 0, j)),
        ],
        out_specs=pl.BlockSpec((1, r, tn), lambda l, j: (l, 0, j)),
        out_shape=jax.ShapeDtypeStruct((depth, r, d3), F32),
        compiler_params=_cparams(("arbitrary", "arbitrary")),
        name="ada_proj",
    )(cond, w_ada, b_ada.reshape(depth, 1, d3))


def _modulated(x, mod_ref, d):
    return _ln(x) * (1.0 + mod_ref[0, :, d:2 * d]) + mod_ref[0, :, 0:d]


def _modproj_kernel(x_ref, mod_ref, w_ref, b_ref, o_ref, h_ref, *, d, sub):
    for r in range(x_ref.shape[0] // sub):
        rows = slice(r * sub, (r + 1) * sub)
        h = _modulated(x_ref[rows, :], mod_ref, d).astype(BF16)
        h_ref[rows, :] = h
        o_ref[rows, :] = (_dot(h, w_ref[...]) + b_ref[...]).astype(o_ref.dtype)


def _mod_proj(x2, mod, w, b, ncols, tm, dtype):
    t, d = x2.shape
    tiles_per_mod = (t // mod.shape[0]) // tm
    return pl.pallas_call(
        functools.partial(_modproj_kernel, d=d, sub=_pick_tile(tm, 128)),
        grid=(t // tm,),
        in_specs=[
            pl.BlockSpec((tm, d), lambda i: (i, 0)),
            _mod_spec(d, tiles_per_mod),
            pl.BlockSpec((d, ncols), lambda i: (0, 0)),
            pl.BlockSpec((1, ncols), lambda i: (0, 0)),
        ],
        out_specs=[pl.BlockSpec((tm, ncols), lambda i: (i, 0)), pl.BlockSpec((tm, d), lambda i: (i, 0))],
        out_shape=[jax.ShapeDtypeStruct((t, ncols), dtype), jax.ShapeDtypeStruct((t, d), BF16)],
        compiler_params=_cparams(("parallel",)),
        name="mod_proj",
    )(x2, mod, w, b)


def _inproj_kernel(h_ref, w_ref, b_ref, o_ref):
    o_ref[...] = (_dot(h_ref[...], w_ref[...]) + b_ref[...]).astype(o_ref.dtype)


def _in_proj(h2, w, b, ncols, tm, tn, dtype):
    t, d = h2.shape
    return pl.pallas_call(
        _inproj_kernel,
        grid=(t // tm, ncols // tn),
        in_specs=[
            pl.BlockSpec((tm, d), lambda i, j: (i, 0)),
            pl.BlockSpec((d, tn), lambda i, j: (0, j)),
            pl.BlockSpec((1, tn), lambda i, j: (0, j)),
        ],
        out_specs=pl.BlockSpec((tm, tn), lambda i, j: (i, j)),
        out_shape=jax.ShapeDtypeStruct((t, ncols), dtype),
        compiler_params=_cparams(("parallel", "arbitrary")),
        name="in_proj",
    )(h2, w, b)


def _sgu_kernel(uv_ref, g_ref, w_ref, b_ref, o_ref, *, wbr, chunk, groups):
    uv = jax.nn.gelu(uv_ref[0])
    u = uv[:, :wbr]
    v = _ln(uv[:, wbr:]).astype(BF16)
    gate = _silu(g_ref[0])
    gd = wbr // groups
    for k in range(uv.shape[0] // chunk):
        rows = slice(k * chunk, (k + 1) * chunk)
        for g in range(groups):
            cols = slice(g * gd, (g + 1) * gd)
            mixed = _dot(w_ref[g], v[rows, cols]) + b_ref[g]
            o_ref[0, rows, cols] = (u[rows, cols] * mixed * gate[rows, cols]).astype(BF16)


def _sgu(z3, w_sgu, b_sgu, wbr, uv_blk, gate_blk, ta):
    bsz, n, _ = z3.shape
    groups, chunk, _ = w_sgu.shape
    return pl.pallas_call(
        functools.partial(_sgu_kernel, wbr=wbr, chunk=chunk, groups=groups),
        grid=(bsz, n // ta),
        in_specs=[
            pl.BlockSpec((1, ta, 2 * wbr), lambda b, i: (b, i, uv_blk)),
            pl.BlockSpec((1, ta, wbr), lambda b, i: (b, i, gate_blk)),
            pl.BlockSpec((groups, chunk, chunk), lambda b, i: (0, 0, 0)),
            pl.BlockSpec((groups, chunk, 1), lambda b, i: (0, 0, 0)),
        ],
        out_specs=pl.BlockSpec((1, ta, wbr), lambda b, i: (b, i, 0)),
        out_shape=jax.ShapeDtypeStruct((bsz, n, wbr), BF16),
        compiler_params=_cparams(("parallel", "arbitrary")),
        name="sgu_branch",
    )(z3, z3, w_sgu.astype(BF16), b_sgu.reshape(groups, chunk, 1))


def _dft_tables(n):
    k = jnp.arange(n, dtype=jnp.int32)
    r = (k[:, None] * k[None, :]) % n
    ang = r.astype(F32) * (2.0 * math.pi / n)
    return jnp.cos(ang).astype(BF16), jnp.sin(ang).astype(BF16)


def _fnet_kernel(z_ref, g_ref, cc_ref, sc_ref, cn_ref, sn_ref, wf_ref, bf_ref, o_ref, p_ref, q_ref,
                 *, groups, norm):
    wbr = z_ref.shape[2]
    gd = wbr // groups

    @pl.when(pl.program_id(1) == 0)
    def _():
        for g in range(groups):
            cols = slice(g * gd, (g + 1) * gd)
            zg = z_ref[0, :, cols]
            p_ref[:, cols] = _dot(zg, cc_ref[...]).astype(BF16)
            q_ref[:, cols] = _dot(zg, sc_ref[...]).astype(BF16)

    f = (_dot(cn_ref[...], p_ref[...]) - _dot(sn_ref[...], q_ref[...])) * norm
    gate = _silu(g_ref[0])
    for g in range(groups):
        cols = slice(g * gd, (g + 1) * gd)
        y = _dot(f[:, cols].astype(BF16), wf_ref[g]) + bf_ref[:, cols]
        o_ref[0, :, cols] = (y * gate[:, cols]).astype(BF16)


def _fnet(zb3, zf3, w_f, b_f, tables, wbr, z_blk, gate_blk, tk):
    bsz, n, _ = zb3.shape
    groups, gd, _ = w_f.shape
    cn, sn, cc, sc = tables
    return pl.pallas_call(
        functools.partial(_fnet_kernel, groups=groups, norm=1.0 / math.sqrt(n * gd)),
        grid=(bsz, n // tk),
        in_specs=[
            pl.BlockSpec((1, n, wbr), lambda b, k: (b, 0, z_blk)),
            pl.BlockSpec((1, tk, wbr), lambda b, k: (b, k, gate_blk)),
            pl.BlockSpec((gd, gd), lambda b, k: (0, 0)),
            pl.BlockSpec((gd, gd), lambda b, k: (0, 0)),
            pl.BlockSpec((tk, n), lambda b, k: (k, 0)),
            pl.BlockSpec((tk, n), lambda b, k: (k, 0)),
            pl.BlockSpec((groups, gd, gd), lambda b, k: (0, 0, 0)),
            pl.BlockSpec((1, wbr), lambda b, k: (0, 0)),
        ],
        out_specs=pl.BlockSpec((1, tk, wbr), lambda b, k: (b, k, 0)),
        out_shape=jax.ShapeDtypeStruct((bsz, n, wbr), BF16),
        scratch_shapes=[pltpu.VMEM((n, wbr), BF16), pltpu.VMEM((n, wbr), BF16)],
        compiler_params=_cparams(("parallel", "arbitrary")),
        name="fnet_branch",
    )(zb3, zf3, cc, sc, cn, sn, w_f.astype(BF16), b_f.reshape(1, wbr))


def _nattn_plan(rows, width, win_h, win_w):
    kh, kw = min(win_h, rows), min(win_w, width)
    qb = Q_ROWS
    wr = min(kh + qb - 1, rows)
    nblk = rows // qb
    r0 = np.arange(nblk) * qb
    ks = np.clip(r0 - kh // 2, 0, rows - wr)
    r = r0[:, None] + np.arange(qb)[None]
    rs = np.clip(r - kh // 2, 0, rows - kh)
    kr = ks[:, None] + np.arange(wr)[None]
    row_ok = (kr[:, None, :] >= rs[:, :, None]) & (kr[:, None, :] < rs[:, :, None] + kh)
    assert row_ok.sum(-1).min() == kh
    dri = np.where(row_ok, kr[:, None, :] - r[:, :, None] + win_h - 1, 0)
    key = np.concatenate([dri.reshape(nblk, -1), row_ok.reshape(nblk, -1)], axis=1)
    _, first, inv = np.unique(key, axis=0, return_index=True, return_inverse=True)
    qc = np.arange(width)
    col_start = np.clip(qc - kw // 2, 0, width - kw)
    in_win = (qc[None, :] >= col_start[:, None]) & (qc[None, :] < col_start[:, None] + kw)
    dci = np.clip(qc[None, :] - qc[:, None], -(kw - 1), kw - 1) + win_w - 1
    return dict(qb=qb, wr=wr, nblk=nblk, ks=ks.astype(np.int32), pat=inv.reshape(-1).astype(np.int32),
                dri=dri[first], row_ok=row_ok[first], in_win=in_win, dci=dci)


def _nattn_bias(rpb, plan, width):
    dri, row_ok = plan["dri"], plan["row_ok"]
    npat, qb, wr = dri.shape
    heads, _, n_dc = rpb.shape
    rows = jnp.take(rpb.astype(F32), jnp.asarray(dri.reshape(-1)), axis=1)
    onehot = (plan["dci"].reshape(-1)[None, :] == np.arange(n_dc)[:, None]).astype(np.float32)
    vals = jnp.einsum("hrc,cw->hrw", rows, jnp.asarray(onehot), precision=HIGHEST)
    vals = vals.reshape(heads, npat, qb, wr, width, width).transpose(1, 0, 2, 4, 3, 5)
    valid = row_ok[:, None, :, None, :, None] & plan["in_win"][None, None, None, :, None, :]
    vals = jnp.where(jnp.asarray(valid), vals, NEG_INF)
    return vals.reshape(npat, heads, qb * width, wr * width)


def _softmax_pv(s_band, s_ctx, v_band, v_ctx):
    m = jnp.maximum(jnp.max(s_band, axis=-1, keepdims=True), jnp.max(s_ctx, axis=-1, keepdims=True))
    p_band = jnp.exp(s_band - m)
    p_ctx = jnp.exp(s_ctx - m)
    den = jnp.sum(p_band, axis=-1, keepdims=True) + jnp.sum(p_ctx, axis=-1, keepdims=True)
    out = _dot(p_band.astype(BF16), v_band) + _dot(p_ctx.astype(BF16), v_ctx)
    return out / den


def _nattn_kernel(ks_ref, pat_ref, q_ref, k_ref, v_ref, ck_ref, cv_ref, g_ref, bias_ref, o_ref,
                  *, heads, width, nkeys):
    blk = pl.program_id(1)
    start = pl.multiple_of(ks_ref[blk] * width, width)
    pat = pat_ref[blk]
    dh = q_ref.shape[2] // heads
    scale = dh ** -0.5
    gate = _silu(g_ref[0])
    head_cols = lambda h: slice(h * dh, (h + 1) * dh)

    def scores(h):
        cols = head_cols(h)
        kb = k_ref[0, pl.ds(start, nkeys), cols]
        s_band = _dot_t(q_ref[0, :, cols], kb) * scale + bias_ref[pat, h]
        s_ctx = _dot_t(q_ref[0, :, cols], ck_ref[0, :, cols]) * scale
        return s_band, s_ctx

    s = scores(0)
    for h in range(heads):
        nxt = scores(h + 1) if h + 1 < heads else None
        cols = head_cols(h)
        vb = v_ref[0, pl.ds(start, nkeys), cols]
        out = _softmax_pv(s[0], s[1], vb, cv_ref[0, :, cols])
        o_ref[0, :, cols] = (out * gate[:, cols]).astype(BF16)
        s = nxt


def _nattn(zxb, zcb, zxf, bias, plan, heads, wbr, blks, gate_blk):
    bsz, n, _ = zxb.shape
    lc = zcb.shape[1]
    k_blk, v_blk, q_blk = blks
    nq = plan["qb"] * GRID_W
    nkeys = plan["wr"] * GRID_W
    grid_spec = pltpu.PrefetchScalarGridSpec(
        num_scalar_prefetch=2,
        grid=(bsz, plan["nblk"]),
        in_specs=[
            pl.BlockSpec((1, nq, wbr), lambda b, i, *_: (b, i, q_blk)),
            pl.BlockSpec((1, n, wbr), lambda b, i, *_: (b, 0, k_blk)),
            pl.BlockSpec((1, n, wbr), lambda b, i, *_: (b, 0, v_blk)),
            pl.BlockSpec((1, lc, wbr), lambda b, i, *_: (b, 0, k_blk)),
            pl.BlockSpec((1, lc, wbr), lambda b, i, *_: (b, 0, v_blk)),
            pl.BlockSpec((1, nq, wbr), lambda b, i, *_: (b, i, gate_blk)),
            pl.BlockSpec(bias.shape, lambda b, i, *_: (0, 0, 0, 0)),
        ],
        out_specs=pl.BlockSpec((1, nq, wbr), lambda b, i, *_: (b, i, 0)),
    )
    return pl.pallas_call(
        functools.partial(_nattn_kernel, heads=heads, width=GRID_W, nkeys=nkeys),
        grid_spec=grid_spec,
        out_shape=jax.ShapeDtypeStruct((bsz, n, wbr), BF16),
        compiler_params=_cparams(("parallel", "arbitrary")),
        name="nattn_branch",
    )(jnp.asarray(plan["ks"]), jnp.asarray(plan["pat"]), zxb, zxb, zxb, zcb, zcb, zxf, bias)


def _cattn_kernel(q_ref, k_ref, v_ref, g_ref, o_ref, *, heads):
    dh = q_ref.shape[2] // heads
    scale = dh ** -0.5
    gate = _silu(g_ref[0])
    for h in range(heads):
        cols = slice(h * dh, (h + 1) * dh)
        s = _dot_t(q_ref[0, :, cols], k_ref[0, :, cols]) * scale
        p = jnp.exp(s - jnp.max(s, axis=-1, keepdims=True))
        out = _dot(p.astype(BF16), v_ref[0, :, cols]) / jnp.sum(p, axis=-1, keepdims=True)
        o_ref[0, :, cols] = (out * gate[:, cols]).astype(BF16)


def _cattn(zcb, zcf, heads, wbr, blks, gate_blk):
    bsz, lc, _ = zcb.shape
    k_blk, v_blk, q_blk = blks
    spec = lambda blk: pl.BlockSpec((1, lc, wbr), lambda b: (b, 0, blk))
    return pl.pallas_call(
        functools.partial(_cattn_kernel, heads=heads),
        grid=(bsz,),
        in_specs=[spec(q_blk), spec(k_blk), spec(v_blk), spec(gate_blk)],
        out_specs=spec(0),
        out_shape=jax.ShapeDtypeStruct((bsz, lc, wbr), BF16),
        compiler_params=_cparams(("parallel",)),
        name="cattn_branch",
    )(zcb, zcb, zcb, zcf)


def _s5_discretise(a_re, a_im, log_dt, b_re, b_im):
    dt = jnp.exp(log_dt)[..., None]
    lr, li = a_re * dt, a_im * dt
    mag = jnp.exp(lr)
    abar_re, abar_im = mag * jnp.cos(li), mag * jnp.sin(li)
    den = a_re * a_re + a_im * a_im
    f_re = ((abar_re - 1.0) * a_re + abar_im * a_im) / den
    f_im = (abar_im * a_re - (abar_re - 1.0) * a_im) / den
    bb_re = f_re[..., None] * b_re - f_im[..., None] * b_im
    bb_im = f_re[..., None] * b_im + f_im[..., None] * b_re
    return abar_re, abar_im, bb_re, bb_im


def _block_diag(m, gpb):
    two, g, a, b = m.shape
    nb = g // gpb
    m = m.reshape(two, nb, gpb, a, b)
    eye = jnp.eye(gpb, dtype=m.dtype)
    out = m[:, :, :, :, None, :] * eye[None, None, :, None, :, None]
    return out.reshape(two, nb, gpb * a, gpb * b)


def _s5_params(a_re, a_im, log_dt, b_re, b_im, c_re, c_im):
    f32 = lambda t: t.astype(F32)
    a_re, a_im, log_dt, b_re, b_im, c_re, c_im = map(f32, (a_re, a_im, log_dt, b_re, b_im, c_re, c_im))
    _, g, p, h = b_re.shape
    gpb = min(g, max(1, LANES // h))
    abar_re, abar_im, bb_re, bb_im = _s5_discretise(a_re, a_im, log_dt, b_re, b_im)
    tr = lambda t: jnp.swapaxes(t, -1, -2)
    bmat = jnp.concatenate([_block_diag(tr(bb_re), gpb), _block_diag(tr(bb_im), gpb)], axis=-1)
    cmat = jnp.stack([_block_diag(tr(c_re), gpb), _block_diag(-tr(c_im), gpb)], axis=2)
    abar = jnp.concatenate([abar_re.reshape(2, 1, g * p), abar_im.reshape(2, 1, g * p)], axis=-1)
    return dict(bmat=bmat.astype(BF16), cmat=cmat.astype(BF16), abar=abar)


def _interleave_perm(nseq, rows):
    src = np.arange(nseq * rows)
    dst = (src % rows) * nseq + src // rows
    pm = np.zeros((nseq * rows, nseq * rows), np.float32)
    pm[dst, src] = 1.0
    return pm


def _cmul_add(ar, ai, xr, xi, br, bi):
    return ar * xr - ai * xi + br, ar * xi + ai * xr + bi


def _permute_rows_exact(pm, y):
    hi = y.astype(BF16)
    r1 = y - hi.astype(F32)
    mid = r1.astype(BF16)
    lo = (r1 - mid.astype(F32)).astype(BF16)
    return _dot(pm, hi) + _dot(pm, mid) + _dot(pm, lo)


def _s5_kernel(*refs, reverse, final):
    if final:
        (u_ref, h0_ref, pm_ref, bm_ref, cm_ref, a_ref, pmt_ref, yf_ref, g_ref, d_ref, wg_ref, bg_ref,
         o_ref, ht_ref, xs_ref, xb_ref, st_ref) = refs
    else:
        (u_ref, h0_ref, pm_ref, bm_ref, cm_ref, a_ref,
         o_ref, ht_ref, xs_ref, xb_ref, st_ref) = refs
    nseq, tcb, wbr = u_ref.shape
    n_tiles, lt = xs_ref.shape[0] // 2, xs_ref.shape[2]
    sp = n_tiles * lt
    nb = bm_ref.shape[1]
    bw_in, bw_st = wbr // nb, sp // nb
    tpb = bw_st // lt
    re_cols = lambda t: slice(t * lt, (t + 1) * lt)
    im_cols = lambda t: slice(sp + t * lt, sp + (t + 1) * lt)

    @pl.when(pl.program_id(1) == 0)
    def _():
        st_ref[...] = h0_ref[...]

    u = u_ref[...].reshape(nseq * tcb, wbr)
    ub = _dot(pm_ref[...], u.astype(BF16)).astype(BF16)
    def drive(n):
        bu = _dot(ub[:, n * bw_in:(n + 1) * bw_in], bm_ref[0, n])
        for t in range(tpb):
            xs_ref[n * tpb + t] = bu[:, t * lt:(t + 1) * lt]
            xs_ref[n_tiles + n * tpb + t] = bu[:, bw_st + t * lt:bw_st + (t + 1) * lt]

    npairs = tcb // 2
    parts = []
    ahead = 2 if final else 1
    for n in range(min(ahead, nb)):
        drive(n)
    for q in range(nb):
        if q + ahead < nb:
            drive(q + ahead)
        tiles = range(q * tpb, (q + 1) * tpb)
        ar = [jnp.broadcast_to(a_ref[0, :, re_cols(t)], (nseq, lt)) for t in tiles]
        ai = [jnp.broadcast_to(a_ref[0, :, im_cols(t)], (nseq, lt)) for t in tiles]

        def step(i, st):
            jj = npairs - 1 - i if reverse else i
            r0 = pl.multiple_of(jj * 2 * nseq, 2 * nseq)
            out = []
            for idx, t in enumerate(tiles):
                dr, di = xs_ref[t, pl.ds(r0, 2 * nseq), :], xs_ref[n_tiles + t, pl.ds(r0, 2 * nseq), :]
                first, second = (slice(nseq, None), slice(0, nseq)) if reverse else (slice(0, nseq), slice(nseq, None))
                x0 = _cmul_add(ar[idx], ai[idx], st[2 * idx], st[2 * idx + 1], dr[first], di[first])
                x1 = _cmul_add(ar[idx], ai[idx], x0[0], x0[1], dr[second], di[second])
                lo, hi = (x1, x0) if reverse else (x0, x1)
                xb_ref[pl.ds(r0, 2 * nseq), re_cols(t)] = jnp.concatenate([lo[0], hi[0]], axis=0).astype(BF16)
                xb_ref[pl.ds(r0, 2 * nseq), im_cols(t)] = jnp.concatenate([lo[1], hi[1]], axis=0).astype(BF16)
                out += [x1[0], x1[1]]
            return tuple(out)

        init = []
        for t in tiles:
            init += [st_ref[:, re_cols(t)], st_ref[:, im_cols(t)]]
        last = lax.fori_loop(0, npairs, step, tuple(init), unroll=True)
        for idx, t in enumerate(tiles):
            st_ref[:, re_cols(t)] = last[2 * idx]
            st_ref[:, im_cols(t)] = last[2 * idx + 1]
        parts.append(_dot(xb_ref[:, q * bw_st:(q + 1) * bw_st], cm_ref[0, q, 0])
                     + _dot(xb_ref[:, sp + q * bw_st:sp + (q + 1) * bw_st], cm_ref[0, q, 1]))

    ht_ref[...] = st_ref[...]
    y = parts[0] if nb == 1 else jnp.concatenate(parts, axis=-1)

    if final:
        y = _permute_rows_exact(pmt_ref[...], yf_ref[...].reshape(nseq * tcb, wbr) + y)
        tot = jax.nn.gelu(y + d_ref[...] * u)
        glu = jax.nn.sigmoid(_dot(tot.astype(BF16), wg_ref[...]) + bg_ref[...])
        gate = _silu(g_ref[...].reshape(nseq * tcb, wbr))
        o_ref[...] = (tot * glu * gate).astype(BF16).reshape(nseq, tcb, wbr)
    else:
        o_ref[...] = y.reshape(nseq, tcb, wbr)


def _s5_pass(z3, h0, prm, dirn, wbr, u_blk, tcb, final_args=None):
    bsz, n, _ = z3.shape
    assert bsz % SUBLANES == 0 and n % tcb == 0 and tcb % 2 == 0
    nchunks = n // tcb
    reverse = dirn == 1
    final = final_args is not None
    sp2 = prm["abar"].shape[-1]
    bmat, cmat = prm["bmat"], prm["cmat"]
    sp = sp2 // 2
    lt = min(LANES, sp // bmat.shape[1])
    rows = SUBLANES * tcb
    pm = _interleave_perm(SUBLANES, tcb)
    chunk = (lambda k: nchunks - 1 - k) if reverse else (lambda k: k)
    seq_spec = lambda blk: pl.BlockSpec((SUBLANES, tcb, wbr), lambda g, k: (g, chunk(k), blk))
    const = lambda shape: pl.BlockSpec(shape, lambda g, k: (0,) * len(shape))
    dir_spec = lambda shape: pl.BlockSpec((1,) + shape[1:], lambda g, k: (dirn,) + (0,) * (len(shape) - 1))
    in_specs = [
        seq_spec(u_blk),
        pl.BlockSpec((SUBLANES, sp2), lambda g, k: (g, 0)),
        const((rows, rows)),
        dir_spec(bmat.shape), dir_spec(cmat.shape), dir_spec(prm["abar"].shape),
    ]
    args = [z3, h0, jnp.asarray(pm, BF16), bmat, cmat, prm["abar"]]
    if final:
        yf, gate_blk, d, w_glu, b_glu = final_args
        in_specs += [const((rows, rows)), seq_spec(0), seq_spec(gate_blk),
                     const((1, wbr)), const((wbr, wbr)), const((1, wbr))]
        args += [jnp.asarray(pm.T, BF16), yf, z3, d.reshape(1, wbr).astype(F32), w_glu.astype(BF16),
                 b_glu.reshape(1, wbr).astype(F32)]
    return pl.pallas_call(
        functools.partial(_s5_kernel, reverse=reverse, final=final),
        grid=(bsz // SUBLANES, nchunks),
        in_specs=in_specs,
        out_specs=[seq_spec(0), pl.BlockSpec((SUBLANES, sp2), lambda g, k: (g, 0))],
        out_shape=[
            jax.ShapeDtypeStruct((bsz, n, wbr), BF16 if final else F32),
            jax.ShapeDtypeStruct((bsz, sp2), F32),
        ],
        scratch_shapes=[
            pltpu.VMEM((sp2 // lt, rows, lt), F32),
            pltpu.VMEM((rows, sp2), BF16),
            pltpu.VMEM((SUBLANES, sp2), F32),
        ],
        compiler_params=_cparams(("parallel", "arbitrary")),
        name="s5_final" if final else "s5_fwd",
    )(*args)


def _s5(z3, h0, prm, d, w_glu, b_glu, wbr, u_blk, gate_blk, tcb):
    yf, hf = _s5_pass(z3, h0[0], prm, 0, wbr, u_blk, tcb)
    out, hb = _s5_pass(z3, h0[1], prm, 1, wbr, u_blk, tcb, final_args=(yf, gate_blk, d, w_glu, b_glu))
    return out, jnp.stack([hf, hb])


def _merge_kernel(h_ref, y0, y1, y2, y3, w0, w1, w2, w3, bm_ref, u0, u1, u2, u3, o_ref):
    h = h_ref[...]
    acc = None
    for i, (y, w, u) in enumerate(zip((y0, y1, y2, y3), (w0, w1, w2, w3), (u0, u1, u2, u3))):
        term = jax.nn.sigmoid(_dot(h, w[...]) + bm_ref[i]) * _dot(y[...], u[0])
        acc = term if acc is None else acc + term
    o_ref[...] = acc.astype(BF16)


def _merge(h2, ys, w_m, b_m, w_up, tm, tn):
    t, d = h2.shape
    wbr = ys[0].shape[1]
    ns = d // tn
    y_spec = pl.BlockSpec((tm, wbr), lambda i, s: (i, 0))
    w_specs = [pl.BlockSpec((d, tn), functools.partial(lambda i, s, br: (0, br * ns + s), br=br))
               for br in range(N_BRANCH)]
    u_specs = [pl.BlockSpec((1, wbr, tn), functools.partial(lambda i, s, br: (br, 0, s), br=br))
               for br in range(N_BRANCH)]
    return pl.pallas_call(
        _merge_kernel,
        grid=(t // tm, ns),
        in_specs=[
            pl.BlockSpec((tm, d), lambda i, s: (i, 0)),
            y_spec, y_spec, y_spec, y_spec,
            *w_specs,
            pl.BlockSpec((N_BRANCH, 1, tn), lambda i, s: (0, 0, s)),
            *u_specs,
        ],
        out_specs=pl.BlockSpec((tm, tn), lambda i, s: (i, s)),
        out_shape=jax.ShapeDtypeStruct((t, d), BF16),
        compiler_params=_cparams(("parallel", "arbitrary")),
        name="merge",
    )(h2, *ys, w_m, w_m, w_m, w_m, b_m, w_up, w_up, w_up, w_up)


def _outproj_kernel(m_ref, x_ref, mod_ref, w_ref, g_ref, b_ref, o_ref, *, d, alpha, sub):
    gate = mod_ref[0, :, 2 * d:3 * d]
    for r in range(m_ref.shape[0] // sub):
        rows = slice(r * sub, (r + 1) * sub)
        out = _dot(m_ref[rows, :], w_ref[...])
        o_ref[rows, :] = _ln(alpha * x_ref[rows, :] + gate * out) * g_ref[...] + b_ref[...]


def _out_proj(m2, x2, mod, w_o, ln_g, ln_b, alpha, tm):
    t, d = x2.shape
    tiles_per_mod = (t // mod.shape[0]) // tm
    row = pl.BlockSpec((tm, d), lambda i: (i, 0))
    vec = pl.BlockSpec((1, d), lambda i: (0, 0))
    return pl.pallas_call(
        functools.partial(_outproj_kernel, d=d, alpha=alpha, sub=_pick_tile(tm, 128)),
        grid=(t // tm,),
        in_specs=[row, row, _mod_spec(d, tiles_per_mod), pl.BlockSpec((d, d), lambda i: (0, 0)), vec, vec],
        out_specs=row,
        out_shape=jax.ShapeDtypeStruct((t, d), F32),
        compiler_params=_cparams(("parallel",)),
        name="out_proj",
    )(m2, x2, mod, w_o, ln_g.reshape(1, d), ln_b.reshape(1, d))


def _outmod_kernel(m_ref, x_ref, mod_ref, modn_ref, wo_ref, g_ref, b_ref, w_ref, bw_ref, xn_ref, h_ref, z_ref,
                   *, d, alpha, sub):
    gate = mod_ref[0, :, 2 * d:3 * d]
    nsub = m_ref.shape[0] // sub
    rows = lambda r: slice(r * sub, (r + 1) * sub)
    out = _dot(m_ref[rows(0), :], wo_ref[...])
    for r in range(nsub):
        nxt = _dot(m_ref[rows(r + 1), :], wo_ref[...]) if r + 1 < nsub else None
        xn = _ln(alpha * x_ref[rows(r), :] + gate * out) * g_ref[...] + b_ref[...]
        xn_ref[rows(r), :] = xn
        h = _modulated(xn, modn_ref, d).astype(BF16)
        h_ref[rows(r), :] = h
        z_ref[rows(r), :] = (_dot(h, w_ref[...]) + bw_ref[...]).astype(z_ref.dtype)
        out = nxt


def _out_mod_proj(m2, x2, mod, mod_next, w_o, ln_g, ln_b, alpha, w, b, ncols, tm):
    t, d = x2.shape
    tiles_per_mod = (t // mod.shape[0]) // tm
    row = lambda width: pl.BlockSpec((tm, width), lambda i: (i, 0))
    vec = lambda width: pl.BlockSpec((1, width), lambda i: (0, 0))
    resident = lambda shape: pl.BlockSpec(shape, lambda i: (0, 0), pipeline_mode=pl.Buffered(1))
    return pl.pallas_call(
        functools.partial(_outmod_kernel, d=d, alpha=alpha, sub=_pick_tile(tm, 128)),
        grid=(t // tm,),
        in_specs=[row(d), row(d), _mod_spec(d, tiles_per_mod), _mod_spec(d, tiles_per_mod),
                  resident((d, d)), vec(d), vec(d), resident((d, ncols)), vec(ncols)],
        out_specs=[row(d), row(d), row(ncols)],
        out_shape=[jax.ShapeDtypeStruct((t, d), F32), jax.ShapeDtypeStruct((t, d), BF16),
                   jax.ShapeDtypeStruct((t, ncols), BF16)],
        compiler_params=_cparams(("parallel",)),
        name="out_mod_proj",
    )(m2, x2, mod, mod_next, w_o, ln_g.reshape(1, d), ln_b.reshape(1, d), w, b)


def kernel(x, c, ctx, c_ctx, w_ada, b_ada, w_in, b_in, w_sgu, b_sgu, w_fnet, b_fnet, rpb, s5_a_re, s5_a_im,
           s5_log_dt, s5_b_re, s5_b_im, s5_c_re, s5_c_im, s5_d, s5_w_glu, s5_b_glu, w_up, w_o, ln_g, ln_b):
    bsz, n, d = x.shape
    lc = ctx.shape[1]
    depth = w_ada.shape[0]
    wbr = d // N_BRANCH
    heads = rpb.shape[1]
    win_h, win_w = (rpb.shape[2] + 1) // 2, (rpb.shape[3] + 1) // 2
    rows = n // GRID_W
    alpha = (2 * depth) ** 0.25
    sp2 = 2 * s5_a_re.shape[2] * s5_a_re.shape[3]

    W = wbr
    take = lambda a, spans: jnp.concatenate([a[..., lo * W:hi * W] for lo, hi in spans], axis=-1)
    spans16 = ((1, 4), (6, 7))
    spans32 = ((4, 6), (0, 1), (7, 11))
    off_merge = 11 * W
    blk_k, blk_v, blk_q, blk_b = 0, 1, 2, 3
    blk_uv, blk_u, blk_gate = 0, 2, 3
    n16, n32 = 4 * W, 7 * W

    n_rows = -(-(bsz + 1) // 8) * 8
    cond = jnp.zeros((n_rows, d), F32).at[:bsz].set(c).at[bsz].set(c_ctx)
    ada = _ada_all(cond, w_ada, b_ada)
    mod_x = [ada[l, :bsz].reshape(bsz, 1, 3 * d) for l in range(depth)]
    mod_c = [ada[l, bsz:bsz + 1].reshape(1, 1, 3 * d) for l in range(depth)]

    w16 = take(w_in, spans16).astype(BF16)
    w32 = take(w_in, spans32).astype(BF16)
    w_m = w_in[:, :, off_merge:].astype(BF16)
    w_up_b = w_up.astype(BF16)
    w_o_b = w_o.astype(BF16)

    plan = _nattn_plan(rows, GRID_W, win_h, win_w)
    gd_f = w_fnet.shape[2]
    cc, sc = _dft_tables(gd_f)
    dft_x = _dft_tables(n) + (cc, sc)
    dft_c = _dft_tables(lc) + (cc, sc)

    tcb_x, tcb_c = _pick_tile(n, S5_ROWS), _pick_tile(lc, S5_ROWS)
    tx, tc_all = bsz * n, bsz * lc
    tm_x, tm_c = _pick_tile(n, TM_MATMUL), _pick_tile(tc_all, TM_MATMUL)
    tm_o_x, tm_o_c = _pick_tile(n, TM_OUT), _pick_tile(tc_all, TM_OUT)
    tn32 = n32 // 2 if (n32 // 2) % LANES == 0 else W

    xt = x.reshape(tx, d)
    ct = ctx.reshape(tc_all, d)
    b16 = [take(b_in[l], spans16).reshape(1, n16) for l in range(depth)]
    ctx_cols16 = lambda l: n16 if l < depth - 1 else 2 * W
    zxb, hx = _mod_proj(xt, mod_x[0], w16[0], b16[0], n16, tm_o_x, BF16)
    zcb, hc = _mod_proj(ct, mod_c[0], w16[0], b16[0], ctx_cols16(0), tm_o_c, BF16)
    for l in range(depth):
        need_ctx = l < depth - 1
        b32 = take(b_in[l], spans32).reshape(1, n32)
        b_m = b_in[l, off_merge:].reshape(N_BRANCH, 1, d)

        zxb = zxb.reshape(bsz, n, n16)
        zxf = _in_proj(hx, w32[l], b32, n32, tm_x, tn32, F32).reshape(bsz, n, n32)
        c16, c32 = (n16, n32) if need_ctx else (2 * W, 3 * W)
        zcb = zcb.reshape(bsz, lc, c16)
        zcf = _in_proj(hc, w32[l], b32, c32, tm_c, tn32 if need_ctx else W, F32).reshape(bsz, lc, c32)

        prm = _s5_params(s5_a_re[l], s5_a_im[l], s5_log_dt[l], s5_b_re[l], s5_b_im[l], s5_c_re[l], s5_c_im[l])
        gate_c = blk_gate + 3 if need_ctx else blk_u
        yd_c, h_ctx = _s5(zcf, jnp.zeros((2, bsz, sp2), F32), prm, s5_d[l], s5_w_glu[l], s5_b_glu[l],
                          W, blk_u, gate_c, tcb_c)
        yd_x, _ = _s5(zxf, h_ctx, prm, s5_d[l], s5_w_glu[l], s5_b_glu[l], W, blk_u, blk_gate + 3, tcb_x)

        bias = _nattn_bias(rpb[l], plan, GRID_W)
        yc_x = _nattn(zxb, zcb, zxf, bias, plan, heads, W, (blk_k, blk_v, blk_q), blk_gate + 2)
        ya_x = _sgu(zxf, w_sgu[l], b_sgu[l], W, blk_uv, blk_gate + 0, _pick_tile(n, T_SGU))
        yb_x = _fnet(zxb, zxf, w_fnet[l], b_fnet[l], dft_x, W, blk_b, blk_gate + 1, _pick_tile(n, T_FNET))

        ys = [y.reshape(tx, W) for y in (ya_x, yb_x, yc_x, yd_x)]
        m = _merge(hx, ys, w_m[l], b_m, w_up_b[l], tm_x, _pick_tile(d, TN_MERGE))
        if need_ctx:
            xt, hx, zxb = _out_mod_proj(m, xt, mod_x[l], mod_x[l + 1], w_o_b[l], ln_g[l], ln_b[l], alpha,
                                        w16[l + 1], b16[l + 1], n16, tm_o_x)
        else:
            xt = _out_proj(m, xt, mod_x[l], w_o_b[l], ln_g[l], ln_b[l], alpha, tm_o_x)

        if need_ctx:
            yc_c = _cattn(zcb, zcf, heads, W, (blk_k, blk_v, blk_q), blk_gate + 2)
            ya_c = _sgu(zcf, w_sgu[l], b_sgu[l], W, blk_uv, blk_gate + 0, _pick_tile(lc, T_SGU))
            yb_c = _fnet(zcb, zcf, w_fnet[l], b_fnet[l], dft_c, W, blk_b, blk_gate + 1, _pick_tile(lc, T_FNET))
            ys_c = [y.reshape(tc_all, W) for y in (ya_c, yb_c, yc_c, yd_c)]
            m_c = _merge(hc, ys_c, w_m[l], b_m, w_up_b[l], tm_c, _pick_tile(d, TN_MERGE))
            ct, hc, zcb = _out_mod_proj(m_c, ct, mod_c[l], mod_c[l + 1], w_o_b[l], ln_g[l], ln_b[l], alpha,
                                        w16[l + 1], b16[l + 1], ctx_cols16(l + 1), tm_o_c)
    return xt.reshape(bsz, n, d)
```
